```python
import math
import jax, jax.numpy as jnp
from jax import lax
import numpy as np

D_MODEL = 1024
BATCH = 4
SEQ = 4096
DEPTH = 2

N_A_LAYERS = DEPTH // 2
N_B_LAYERS = DEPTH - N_A_LAYERS
HEAD_DIM = 64
N_HEADS = D_MODEL // (2 * HEAD_DIM)
D_FF = 2816
CONV_WIDTH = 3
Q_BLOCK = 128
NORM_EPS = 1e-6

kernel_name = "yoco_shortconv_diffattn_macaron"


def rmsnorm(x, g):
    xf = x.astype(jnp.float32)
    y = xf * lax.rsqrt(jnp.mean(xf * xf, axis=-1, keepdims=True) + NORM_EPS)
    return (y * g.astype(jnp.float32)).astype(x.dtype)


def swiglu(x, w_gu, w_down):
    gate, up = jnp.split(x @ w_gu, 2, axis=-1)
    return (jax.nn.silu(gate) * up) @ w_down


def short_gated_conv(xn, w_in, conv_k, w_out):
    b_gate, c_gate, z = jnp.split(xn @ w_in, 3, axis=-1)
    u = c_gate * z
    rhs = conv_k[:, None, :].astype(u.dtype)
    conv = lax.conv_general_dilated(
        u, rhs, window_strides=(1,), padding=[(CONV_WIDTH - 1, 0)],
        dimension_numbers=('NWC', 'WIO', 'NWC'), feature_group_count=u.shape[-1])
    return (b_gate * conv) @ w_out


def alibi_slopes(n_heads):
    return 2.0 ** (-8.0 * jnp.arange(1, n_heads + 1, dtype=jnp.float32) / n_heads)


def shared_kv(h, kv_norm_g, w_kv):
    Bsz, S, _ = h.shape
    k_flat, v_flat = jnp.split(rmsnorm(h, kv_norm_g) @ w_kv, 2, axis=-1)
    k = k_flat.reshape(Bsz, S, N_HEADS, 2, HEAD_DIM)
    v = v_flat.reshape(Bsz, S, N_HEADS, 2 * HEAD_DIM)
    return k, v


def diff_attention(xn, k, v, w_q, lam_params, subln_g, w_o, lambda_init):
    Bsz, S, _ = xn.shape
    q = (xn @ w_q).reshape(Bsz, S, N_HEADS, 2, HEAD_DIM)
    lp = lam_params.astype(jnp.float32)
    lam = jnp.exp(jnp.sum(lp[0] * lp[1])) - jnp.exp(jnp.sum(lp[2] * lp[3])) + lambda_init
    scale = HEAD_DIM ** -0.5
    n_blocks = S // Q_BLOCK
    q_blocks = q.reshape(Bsz, n_blocks, Q_BLOCK, N_HEADS, 2, HEAD_DIM).swapaxes(0, 1)
    slopes = alibi_slopes(N_HEADS)
    kpos = jnp.arange(S)

    def one_block(args):
        q_blk, start = args
        qpos = start + jnp.arange(Q_BLOCK)
        dist = (qpos[:, None] - kpos[None, :]).astype(jnp.float32)
        s = jnp.einsum('bqhcd,bkhcd->bhcqk', q_blk, k,
                       preferred_element_type=jnp.float32) * scale
        s = s - slopes[None, :, None, None, None] * dist
        s = jnp.where(dist >= 0, s, -jnp.inf)
        p = jax.nn.softmax(s, axis=-1)
        a = p[:, :, 0] - lam * p[:, :, 1]
        return jnp.einsum('bhqk,bkhe->bqhe', a, v)

    o = lax.map(one_block, (q_blocks, jnp.arange(n_blocks) * Q_BLOCK))
    o = o.swapaxes(0, 1).reshape(Bsz, S, N_HEADS, 2 * HEAD_DIM).astype(jnp.float32)
    o = o * lax.rsqrt(jnp.mean(o * o, axis=-1, keepdims=True) + NORM_EPS)
    o = o * subln_g.astype(jnp.float32) * (1.0 - lambda_init)
    return o.reshape(Bsz, S, N_HEADS * 2 * HEAD_DIM).astype(xn.dtype) @ w_o


def setup_inputs(seed: int = 0) -> dict:
    key = jax.random.key(seed)
    ks = jax.random.split(key, 16)
    D = D_MODEL
    nrm = lambda k, shape, fan_in: jax.random.normal(k, shape, jnp.float32) * fan_in ** -0.5
    x = jax.random.normal(ks[0], (BATCH, SEQ, D), jnp.float32)
    ffn_w_gu = nrm(ks[1], (DEPTH, 2, D, 2 * D_FF), D)
    ffn_w_down = nrm(ks[2], (DEPTH, 2, D_FF, D), D_FF)
    norm_g = 1.0 + 0.05 * jax.random.normal(ks[3], (DEPTH, 6, D), jnp.float32)
    conv_w_in = nrm(ks[4], (N_A_LAYERS, D, 3 * D), D)
    conv_k = nrm(ks[5], (N_A_LAYERS, CONV_WIDTH, D), CONV_WIDTH)
    conv_w_out = nrm(ks[6], (N_A_LAYERS, D, D), D)
    kv_norm_g = 1.0 + 0.05 * jax.random.normal(ks[7], (D,), jnp.float32)
    w_kv = nrm(ks[8], (D, 2 * D), D)
    attn_w_q = nrm(ks[9], (N_B_LAYERS, D, D), D)
    attn_lambda = 0.1 * jax.random.normal(ks[10], (N_B_LAYERS, 4, HEAD_DIM), jnp.float32)
    attn_subln_g = 1.0 + 0.05 * jax.random.normal(ks[11], (N_B_LAYERS, 2 * HEAD_DIM), jnp.float32)
    attn_w_o = nrm(ks[12], (N_B_LAYERS, D, D), D)
    return {"x": x, "ffn_w_gu": ffn_w_gu, "ffn_w_down": ffn_w_down, "norm_g": norm_g,
            "conv_w_in": conv_w_in, "conv_k": conv_k, "conv_w_out": conv_w_out,
            "kv_norm_g": kv_norm_g, "w_kv": w_kv, "attn_w_q": attn_w_q,
            "attn_lambda": attn_lambda, "attn_subln_g": attn_subln_g, "attn_w_o": attn_w_o}


def reference(x, ffn_w_gu, ffn_w_down, norm_g, conv_w_in, conv_k, conv_w_out,
              kv_norm_g, w_kv, attn_w_q, attn_lambda, attn_subln_g, attn_w_o):
    h = x
    k = v = None
    for l in range(DEPTH):
        g = norm_g[l]
        if l == N_A_LAYERS:
            k, v = shared_kv(h, kv_norm_g, w_kv)
        h = h + 0.5 * rmsnorm(swiglu(rmsnorm(h, g[0]), ffn_w_gu[l, 0], ffn_w_down[l, 0]), g[1])
        xn = rmsnorm(h, g[2])
        if l < N_A_LAYERS:
            mix = short_gated_conv(xn, conv_w_in[l], conv_k[l], conv_w_out[l])
        else:
            j = l - N_A_LAYERS
            lambda_init = 0.8 - 0.6 * math.exp(-0.3 * l)
            mix = diff_attention(xn, k, v, attn_w_q[j], attn_lambda[j],
                                 attn_subln_g[j], attn_w_o[j], lambda_init)
        h = h + rmsnorm(mix, g[3])
        h = h + 0.5 * rmsnorm(swiglu(rmsnorm(h, g[4]), ffn_w_gu[l, 1], ffn_w_down[l, 1]), g[5])
    return h
```

```python
import functools
import math

import jax
import jax.numpy as jnp
from jax import lax
from jax.experimental import pallas as pl
from jax.experimental.pallas import tpu as pltpu

D_MODEL = 1024
D_FF = 2816
HEAD_DIM = 64
HEAD_WIDTH = 2 * HEAD_DIM
N_HEADS = D_MODEL // HEAD_WIDTH
CONV_WIDTH = 3
NORM_EPS = 1e-6
N_A_LAYERS = 1

ROW_TILE = 512
FF_CHUNK = 256
CONV_CHUNK = 256
CONV_HALO = 8
Q_TILE = 512
KV_TILE = 512
VMEM_LIMIT_BYTES = 56 * 1024 * 1024

_F32 = jnp.float32
_BF16 = jnp.bfloat16


def _rms(x, g):
    return x * lax.rsqrt(jnp.mean(x * x, axis=-1, keepdims=True) + NORM_EPS) * g


def _dot(a, b):
    return jnp.dot(a, b, preferred_element_type=_F32)


def _dot_nt(a, b):
    return lax.dot_general(a, b, (((1,), (1,)), ((), ())), preferred_element_type=_F32)


def _resident(shape):
    return pl.BlockSpec(shape, lambda *_: (0,) * len(shape), pipeline_mode=pl.Buffered(1))


def _row_params(semantics):
    return pltpu.CompilerParams(dimension_semantics=semantics, vmem_limit_bytes=VMEM_LIMIT_BYTES)


def _ffn_kernel(h_ref, gpre_ref, gpost_ref, wgu_ref, wd_ref, o_ref):
    h = h_ref[...]
    xn = _rms(h, gpre_ref[...]).astype(_BF16)
    acc = jnp.zeros(h.shape, _F32)
    for c in range(D_FF // FF_CHUNK):
        lo = c * FF_CHUNK
        gate = _dot(xn, wgu_ref[:, lo:lo + FF_CHUNK])
        up = _dot(xn, wgu_ref[:, D_FF + lo:D_FF + lo + FF_CHUNK])
        act = (jax.nn.silu(gate) * up).astype(_BF16)
        acc = acc + _dot(act, wd_ref[lo:lo + FF_CHUNK, :])
    o_ref[...] = h + 0.5 * _rms(acc, gpost_ref[...])


def _ffn(h, g_pre, g_post, w_gu, w_down):
    t, d = h.shape
    row = pl.BlockSpec((ROW_TILE, d), lambda i: (i, 0))
    return pl.pallas_call(
        _ffn_kernel,
        grid=(t // ROW_TILE,),
        in_specs=[row, _resident((1, d)), _resident((1, d)),
                  _resident(w_gu.shape), _resident(w_down.shape)],
        out_specs=row,
        out_shape=jax.ShapeDtypeStruct(h.shape, _F32),
        compiler_params=_row_params(("parallel",)),
        name="ffn_half",
    )(h, g_pre, g_post, w_gu, w_down)


def _conv_kernel(h_ref, gpre_ref, gpost_ref, win_ref, ck_ref, wout_ref, o_ref, u_ref, *, tiles_per_seq):
    tm, d = h_ref.shape
    first_of_seq = pl.program_id(0) % tiles_per_seq == 0

    @pl.when(first_of_seq)
    def _():
        u_ref[0:CONV_HALO, :] = jnp.zeros((CONV_HALO, d), _F32)

    @pl.when(jnp.logical_not(first_of_seq))
    def _():
        u_ref[0:CONV_HALO, :] = u_ref[tm:tm + CONV_HALO, :]

    h = h_ref[...]
    xn = _rms(h, gpre_ref[...]).astype(_BF16)
    acc = jnp.zeros(h.shape, _F32)
    for c in range(d // CONV_CHUNK):
        lo = c * CONV_CHUNK
        cols = slice(lo, lo + CONV_CHUNK)
        b_gate = _dot(xn, win_ref[:, lo:lo + CONV_CHUNK])
        c_gate = _dot(xn, win_ref[:, d + lo:d + lo + CONV_CHUNK])
        z = _dot(xn, win_ref[:, 2 * d + lo:2 * d + lo + CONV_CHUNK])
        u = c_gate * z
        u_ref[CONV_HALO:CONV_HALO + tm, cols] = u
        conv = (ck_ref[2:3, cols] * u
                + ck_ref[1:2, cols] * u_ref[CONV_HALO - 1:CONV_HALO - 1 + tm, cols]
                + ck_ref[0:1, cols] * u_ref[CONV_HALO - 2:CONV_HALO - 2 + tm, cols])
        gated = (b_gate * conv).astype(_BF16)
        acc = acc + _dot(gated, wout_ref[lo:lo + CONV_CHUNK, :])
    o_ref[...] = h + _rms(acc, gpost_ref[...])


def _conv_mixer(h, g_pre, g_post, w_in, conv_k, w_out, seq_len):
    t, d = h.shape
    row = pl.BlockSpec((ROW_TILE, d), lambda i: (i, 0))
    kern = functools.partial(_conv_kernel, tiles_per_seq=seq_len // ROW_TILE)
    return pl.pallas_call(
        kern,
        grid=(t // ROW_TILE,),
        in_specs=[row, _resident((1, d)), _resident((1, d)), _resident(w_in.shape),
                  _resident(conv_k.shape), _resident(w_out.shape)],
        out_specs=row,
        out_shape=jax.ShapeDtypeStruct(h.shape, _F32),
        scratch_shapes=[pltpu.VMEM((ROW_TILE + CONV_HALO, d), _F32)],
        compiler_params=_row_params(("arbitrary",)),
        name="conv_mixer",
    )(h, g_pre, g_post, w_in, conv_k, w_out)


def _kv_kernel(h_ref, g_ref, wk_ref, wvt_ref, k_ref, vt_ref):
    xn = _rms(h_ref[...], g_ref[...]).astype(_BF16)
    k_ref[...] = _dot(xn, wk_ref[...]).astype(_BF16)
    vt_ref[0, 0] = _dot_nt(wvt_ref[...], xn).astype(_BF16)


def _kv_proj(h, g, w_k, w_vt, seq_len):
    t, d = h.shape
    n_kv = seq_len // KV_TILE
    row = pl.BlockSpec((KV_TILE, d), lambda i: (i, 0))
    return pl.pallas_call(
        _kv_kernel,
        grid=(t // KV_TILE,),
        in_specs=[row, _resident((1, d)), _resident(w_k.shape), _resident(w_vt.shape)],
        out_specs=[row, pl.BlockSpec((1, 1, d, KV_TILE), lambda i: (i // n_kv, i % n_kv, 0, 0))],
        out_shape=[jax.ShapeDtypeStruct((t, d), _BF16),
                   jax.ShapeDtypeStruct((t // seq_len, n_kv, d, KV_TILE), _BF16)],
        compiler_params=_row_params(("parallel",)),
        name="kv_proj",
    )(h, g, w_k, w_vt)


def _q_kernel(h_ref, g_ref, wq_ref, q_ref):
    xn = _rms(h_ref[...], g_ref[...]).astype(_BF16)
    q_ref[...] = (_dot(xn, wq_ref[...]) * (HEAD_DIM ** -0.5)).astype(_BF16)


def _q_proj(h, g, w_q):
    t, d = h.shape
    row = pl.BlockSpec((ROW_TILE, d), lambda i: (i, 0))
    return pl.pallas_call(
        _q_kernel,
        grid=(t // ROW_TILE,),
        in_specs=[row, _resident((1, d)), _resident(w_q.shape)],
        out_specs=row,
        out_shape=jax.ShapeDtypeStruct((t, d), _BF16),
        compiler_params=_row_params(("parallel",)),
        name="q_proj",
    )(h, g, w_q)


def _attn_kernel(slopes_ref, q_ref, k_ref, vt_ref, lam_ref, sg_ref, o_ref,
                 bias_ref, bias_diag_ref, acc_ref, *, lambda_init):
    tq = q_ref.shape[1]
    tk = vt_ref.shape[3]
    head = pl.program_id(1)
    qi = pl.program_id(2)
    slope = slopes_ref[head]

    @pl.when(qi == 0)
    def _():
        kk = lax.broadcasted_iota(jnp.int32, (tk, tq), 0)
        qq = lax.broadcasted_iota(jnp.int32, (tk, tq), 1)
        bias = slope * (kk - qq).astype(_F32)
        bias_ref[...] = bias
        bias_diag_ref[...] = jnp.where(kk <= qq, bias, -jnp.inf)

    q = q_ref[0]
    lane = lax.broadcasted_iota(jnp.int32, q.shape, 1)
    zero = jnp.zeros_like(q)
    q_maps = (jnp.where(lane < HEAD_DIM, q, zero), jnp.where(lane >= HEAD_DIM, q, zero))

    acc_ref[...] = jnp.zeros(acc_ref.shape, _F32)

    def block(j, bias, stats):
        kb = k_ref[0, pl.ds(pl.multiple_of(j * tk, tk), tk), :]
        vb = vt_ref[0, j]
        offset = slope * (j * tk - qi * tq).astype(_F32)
        new_stats = []
        for c in range(2):
            m_old, l_old = stats[2 * c], stats[2 * c + 1]
            s = _dot_nt(kb, q_maps[c]) + bias
            m_new = jnp.maximum(m_old, jnp.max(s, axis=0, keepdims=True) + offset)
            alpha = jnp.exp(m_old - m_new)
            p = jnp.exp(s - (m_new - offset))
            l_new = alpha * l_old + jnp.sum(p, axis=0, keepdims=True)
            acc_ref[c] = alpha * acc_ref[c] + _dot(vb, p.astype(_BF16))
            new_stats += [m_new, l_new]
        return tuple(new_stats)

    neg_inf = jnp.full((1, tq), -jnp.inf, _F32)
    zeros = jnp.zeros((1, tq), _F32)
    stats = lax.fori_loop(0, qi, lambda j, st: block(j, bias_ref[...], st),
                          (neg_inf, zeros, neg_inf, zeros))
    _, l0, _, l1 = block(qi, bias_diag_ref[...], stats)

    lp = lam_ref[...]
    lam = (jnp.exp(jnp.sum(lp[0:1] * lp[1:2], axis=-1, keepdims=True))
           - jnp.exp(jnp.sum(lp[2:3] * lp[3:4], axis=-1, keepdims=True)) + lambda_init)
    o = acc_ref[0] / l0 - lam * (acc_ref[1] / l1)
    o = o * lax.rsqrt(jnp.mean(o * o, axis=0, keepdims=True) + NORM_EPS)
    o_ref[0] = (o.T * sg_ref[...] * (1.0 - lambda_init)).astype(o_ref.dtype)


def _diff_attention(q, k, vt, slopes, lam_params, subln_g, lambda_init):
    b, s, d = q.shape
    n_kv = vt.shape[1]
    kern = functools.partial(_attn_kernel, lambda_init=lambda_init)
    qo_spec = pl.BlockSpec((1, Q_TILE, HEAD_WIDTH), lambda bi, hi, qi: (bi, qi, hi))
    return pl.pallas_call(
        kern,
        grid=(b, N_HEADS, s // Q_TILE),
        in_specs=[
            pl.BlockSpec(memory_space=pltpu.SMEM),
            qo_spec,
            pl.BlockSpec((1, s, HEAD_WIDTH), lambda bi, hi, qi: (bi, 0, hi)),
            pl.BlockSpec((1, n_kv, HEAD_WIDTH, KV_TILE), lambda bi, hi, qi: (bi, 0, hi, 0)),
            pl.BlockSpec(lam_params.shape, lambda bi, hi, qi: (0, 0)),
            pl.BlockSpec(subln_g.shape, lambda bi, hi, qi: (0, 0)),
        ],
        out_specs=qo_spec,
        out_shape=jax.ShapeDtypeStruct((b, s, d), _BF16),
        scratch_shapes=[pltpu.VMEM((KV_TILE, Q_TILE), _F32),
                        pltpu.VMEM((KV_TILE, Q_TILE), _F32),
                        pltpu.VMEM((2, HEAD_WIDTH, Q_TILE), _F32)],
        compiler_params=pltpu.CompilerParams(
            dimension_semantics=("arbitrary", "arbitrary", "arbitrary"),
            vmem_limit_bytes=VMEM_LIMIT_BYTES),
        name="diff_attention",
    )(slopes, q, k, vt, lam_params, subln_g)


def _oproj_kernel(h_ref, o_ref, g_ref, wo_ref, out_ref):
    h = h_ref[...]
    out_ref[...] = h + _rms(_dot(o_ref[...], wo_ref[...]), g_ref[...])


def _o_proj(h, o, g, w_o):
    t, d = h.shape
    row = pl.BlockSpec((ROW_TILE, d), lambda i: (i, 0))
    return pl.pallas_call(
        _oproj_kernel,
        grid=(t // ROW_TILE,),
        in_specs=[row, row, _resident((1, d)), _resident(w_o.shape)],
        out_specs=row,
        out_shape=jax.ShapeDtypeStruct(h.shape, _F32),
        compiler_params=_row_params(("parallel",)),
        name="o_proj",
    )(h, o, g, w_o)


def kernel(x, ffn_w_gu, ffn_w_down, norm_g, conv_w_in, conv_k, conv_w_out, kv_norm_g, w_kv,
           attn_w_q, attn_lambda, attn_subln_g, attn_w_o):
    bsz, seq, d = x.shape
    assert d == D_MODEL and seq % ROW_TILE == 0 and seq % KV_TILE == 0 and Q_TILE == KV_TILE
    depth = ffn_w_gu.shape[0]
    assert depth == 2 * N_A_LAYERS

    bf = lambda w: w.astype(_BF16)
    g = lambda l, i: norm_g[l, i].reshape(1, d)
    h = x.reshape(bsz * seq, d)

    h = _ffn(h, g(0, 0), g(0, 1), bf(ffn_w_gu[0, 0]), bf(ffn_w_down[0, 0]))
    h = _conv_mixer(h, g(0, 2), g(0, 3), bf(conv_w_in[0]), conv_k[0], bf(conv_w_out[0]), seq)
    h = _ffn(h, g(0, 4), g(0, 5), bf(ffn_w_gu[0, 1]), bf(ffn_w_down[0, 1]))

    k, vt = _kv_proj(h, kv_norm_g.reshape(1, d), bf(w_kv[:, :d]), bf(w_kv[:, d:].T), seq)

    h = _ffn(h, g(1, 0), g(1, 1), bf(ffn_w_gu[1, 0]), bf(ffn_w_down[1, 0]))
    q = _q_proj(h, g(1, 2), bf(attn_w_q[0]))
    lambda_init = 0.8 - 0.6 * math.exp(-0.3 * 1)
    slopes = 2.0 ** (-8.0 * jnp.arange(1, N_HEADS + 1, dtype=_F32) / N_HEADS)
    o = _diff_attention(q.reshape(bsz, seq, d), k.reshape(bsz, seq, d), vt, slopes,
                        attn_lambda[0], attn_subln_g[0].reshape(1, HEAD_WIDTH), lambda_init)
    h = _o_proj(h, o.reshape(bsz * seq, d), g(1, 3), bf(attn_w_o[0]))
    h = _ffn(h, g(1, 4), g(1, 5), bf(ffn_w_gu[1, 1]), bf(ffn_w_down[1, 1]))
    return h.reshape(bsz, seq, d)
```

```python
import functools
import math

import jax
import jax.numpy as jnp
from jax import lax
from jax.experimental import pallas as pl
from jax.experimental.pallas import tpu as pltpu

D_MODEL = 1024
D_FF = 2816
HEAD_DIM = 64
HEAD_WIDTH = 2 * HEAD_DIM
N_HEADS = D_MODEL // HEAD_WIDTH
CONV_WIDTH = 3
NORM_EPS = 1e-6
N_A_LAYERS = 1

ROW_TILE = 512
FF_CHUNK = 256
CONV_CHUNK = 256
CONV_HALO = 8
Q_TILE = 512
KV_TILE = 512
SUM_ROWS = 16
LOG2_E = math.log2(math.e)
VMEM_LIMIT_BYTES = 56 * 1024 * 1024

_F32 = jnp.float32
_BF16 = jnp.bfloat16


def _rms(x, g):
    return x * lax.rsqrt(jnp.mean(x * x, axis=-1, keepdims=True) + NORM_EPS) * g


def _dot(a, b):
    return jnp.dot(a, b, preferred_element_type=_F32)


def _dot_nt(a, b):
    return lax.dot_general(a, b, (((1,), (1,)), ((), ())), preferred_element_type=_F32)


def _resident(shape):
    return pl.BlockSpec(shape, lambda *_: (0,) * len(shape), pipeline_mode=pl.Buffered(1))


def _row_params(semantics):
    return pltpu.CompilerParams(dimension_semantics=semantics, vmem_limit_bytes=VMEM_LIMIT_BYTES)


def _ffn_kernel(h_ref, gpre_ref, gpost_ref, wgu_ref, wd_ref, o_ref):
    h = h_ref[...]
    xn = _rms(h, gpre_ref[...]).astype(_BF16)
    acc = jnp.zeros(h.shape, _F32)
    for c in range(D_FF // FF_CHUNK):
        lo = c * FF_CHUNK
        gate = _dot(xn, wgu_ref[:, lo:lo + FF_CHUNK])
        up = _dot(xn, wgu_ref[:, D_FF + lo:D_FF + lo + FF_CHUNK])
        act = (jax.nn.silu(gate) * up).astype(_BF16)
        acc = acc + _dot(act, wd_ref[lo:lo + FF_CHUNK, :])
    o_ref[...] = h + 0.5 * _rms(acc, gpost_ref[...])


def _ffn(h, g_pre, g_post, w_gu, w_down):
    t, d = h.shape
    row = pl.BlockSpec((ROW_TILE, d), lambda i: (i, 0))
    return pl.pallas_call(
        _ffn_kernel,
        grid=(t // ROW_TILE,),
        in_specs=[row, _resident((1, d)), _resident((1, d)),
                  _resident(w_gu.shape), _resident(w_down.shape)],
        out_specs=row,
        out_shape=jax.ShapeDtypeStruct(h.shape, _F32),
        compiler_params=_row_params(("parallel",)),
        name="ffn_half",
    )(h, g_pre, g_post, w_gu, w_down)


def _conv_kernel(h_ref, gpre_ref, gpost_ref, win_ref, ck_ref, wout_ref, o_ref, u_ref, *, tiles_per_seq):
    tm, d = h_ref.shape
    first_of_seq = pl.program_id(0) % tiles_per_seq == 0

    @pl.when(first_of_seq)
    def _():
        u_ref[0:CONV_HALO, :] = jnp.zeros((CONV_HALO, d), _F32)

    @pl.when(jnp.logical_not(first_of_seq))
    def _():
        u_ref[0:CONV_HALO, :] = u_ref[tm:tm + CONV_HALO, :]

    h = h_ref[...]
    xn = _rms(h, gpre_ref[...]).astype(_BF16)
    acc = jnp.zeros(h.shape, _F32)
    for c in range(d // CONV_CHUNK):
        lo = c * CONV_CHUNK
        cols = slice(lo, lo + CONV_CHUNK)
        b_gate = _dot(xn, win_ref[:, lo:lo + CONV_CHUNK])
        c_gate = _dot(xn, win_ref[:, d + lo:d + lo + CONV_CHUNK])
        z = _dot(xn, win_ref[:, 2 * d + lo:2 * d + lo + CONV_CHUNK])
        u = c_gate * z
        u_ref[CONV_HALO:CONV_HALO + tm, cols] = u
        conv = (ck_ref[2:3, cols] * u
                + ck_ref[1:2, cols] * u_ref[CONV_HALO - 1:CONV_HALO - 1 + tm, cols]
                + ck_ref[0:1, cols] * u_ref[CONV_HALO - 2:CONV_HALO - 2 + tm, cols])
        gated = (b_gate * conv).astype(_BF16)
        acc = acc + _dot(gated, wout_ref[lo:lo + CONV_CHUNK, :])
    o_ref[...] = h + _rms(acc, gpost_ref[...])


def _conv_mixer(h, g_pre, g_post, w_in, conv_k, w_out, seq_len):
    t, d = h.shape
    row = pl.BlockSpec((ROW_TILE, d), lambda i: (i, 0))
    kern = functools.partial(_conv_kernel, tiles_per_seq=seq_len // ROW_TILE)
    return pl.pallas_call(
        kern,
        grid=(t // ROW_TILE,),
        in_specs=[row, _resident((1, d)), _resident((1, d)), _resident(w_in.shape),
                  _resident(conv_k.shape), _resident(w_out.shape)],
        out_specs=row,
        out_shape=jax.ShapeDtypeStruct(h.shape, _F32),
        scratch_shapes=[pltpu.VMEM((ROW_TILE + CONV_HALO, d), _F32)],
        compiler_params=_row_params(("arbitrary",)),
        name="conv_mixer",
    )(h, g_pre, g_post, w_in, conv_k, w_out)


def _kv_kernel(h_ref, g_ref, wk_ref, wvt_ref, k_ref, vt_ref):
    xn = _rms(h_ref[...], g_ref[...]).astype(_BF16)
    k_ref[...] = _dot(xn, wk_ref[...]).astype(_BF16)
    vt_ref[0, 0] = _dot_nt(wvt_ref[...], xn).astype(_BF16)


def _kv_proj(h, g, w_k, w_vt, seq_len):
    t, d = h.shape
    n_kv = seq_len // KV_TILE
    row = pl.BlockSpec((KV_TILE, d), lambda i: (i, 0))
    return pl.pallas_call(
        _kv_kernel,
        grid=(t // KV_TILE,),
        in_specs=[row, _resident((1, d)), _resident(w_k.shape), _resident(w_vt.shape)],
        out_specs=[row, pl.BlockSpec((1, 1, d, KV_TILE), lambda i: (i // n_kv, i % n_kv, 0, 0))],
        out_shape=[jax.ShapeDtypeStruct((t, d), _BF16),
                   jax.ShapeDtypeStruct((t // seq_len, n_kv, d, KV_TILE), _BF16)],
        compiler_params=_row_params(("parallel",)),
        name="kv_proj",
    )(h, g, w_k, w_vt)


def _q_kernel(h_ref, g_ref, wq_ref, q_ref):
    xn = _rms(h_ref[...], g_ref[...]).astype(_BF16)
    q_ref[...] = (_dot(xn, wq_ref[...]) * (HEAD_DIM ** -0.5 * LOG2_E)).astype(_BF16)


def _q_proj(h, g, w_q):
    t, d = h.shape
    row = pl.BlockSpec((ROW_TILE, d), lambda i: (i, 0))
    return pl.pallas_call(
        _q_kernel,
        grid=(t // ROW_TILE,),
        in_specs=[row, _resident((1, d)), _resident(w_q.shape)],
        out_specs=row,
        out_shape=jax.ShapeDtypeStruct((t, d), _BF16),
        compiler_params=_row_params(("parallel",)),
        name="q_proj",
    )(h, g, w_q)


def _attn_kernel(slopes_ref, q_ref, k_ref, vt_ref, lam_ref, sg_ref, o_ref,
                 bias_ref, s_ref, acc_ref, *, lambda_init):
    tq = q_ref.shape[1]
    tk = vt_ref.shape[3]
    head = pl.program_id(1)
    qi = pl.program_id(2)
    slope = slopes_ref[head] * LOG2_E

    @pl.when(qi == 0)
    def _():
        kk = lax.broadcasted_iota(jnp.int32, (tk, tq), 0)
        qq = lax.broadcasted_iota(jnp.int32, (tk, tq), 1)
        bias = slope * (kk - qq).astype(_F32)
        bias_ref[0] = bias
        bias_ref[1] = jnp.where(kk <= qq, bias, -jnp.inf)

    q = q_ref[0]
    lane = lax.broadcasted_iota(jnp.int32, q.shape, 1)
    zero = jnp.zeros_like(q)
    q_maps = (jnp.where(lane < HEAD_DIM, q, zero), jnp.where(lane >= HEAD_DIM, q, zero))
    ones_rows = jnp.ones((SUM_ROWS, tk), _BF16)

    acc_ref[...] = jnp.zeros(acc_ref.shape, _F32)

    def block_offset(j):
        return slope * (j * tk - qi * tq).astype(_F32)

    def scores(j):
        kb = k_ref[0, pl.ds(pl.multiple_of(j * tk, tk), tk), :]
        bias = bias_ref[(j == qi).astype(jnp.int32)]
        s_pair = [_dot_nt(kb, q_maps[c]) + bias for c in range(2)]
        tops = [jnp.max(s, axis=0, keepdims=True) + block_offset(j) for s in s_pair]
        return s_pair, tops

    def exp_pv(j, s_pair, tops, maxima):
        vb = jnp.concatenate([vt_ref[0, j], ones_rows], axis=0)
        new_maxima = []
        for c in range(2):
            m_new = jnp.maximum(maxima[c], tops[c])
            alpha = jnp.exp2(maxima[c] - m_new)
            p = jnp.exp2(s_pair[c] - (m_new - block_offset(j)))
            acc_ref[c] = alpha * acc_ref[c] + _dot(vb, p.astype(_BF16))
            new_maxima.append(m_new)
        return new_maxima

    first, first_tops = scores(0)
    for c in range(2):
        s_ref[c] = first[c]

    def body(j, carry):
        maxima, tops = carry[0:2], carry[2:4]
        cur = [s_ref[0], s_ref[1]]
        nxt, nxt_tops = scores(j + 1)
        maxima = exp_pv(j, cur, tops, maxima)
        for c in range(2):
            s_ref[c] = nxt[c]
        return (*maxima, *nxt_tops)

    neg_inf = jnp.full((1, tq), -jnp.inf, _F32)
    carry = lax.fori_loop(0, qi, body, (neg_inf, neg_inf, *first_tops))
    exp_pv(qi, [s_ref[0], s_ref[1]], carry[2:4], carry[0:2])

    lp = lam_ref[...]
    lam = (jnp.exp(jnp.sum(lp[0:1] * lp[1:2], axis=-1, keepdims=True))
           - jnp.exp(jnp.sum(lp[2:3] * lp[3:4], axis=-1, keepdims=True)) + lambda_init)
    w = HEAD_WIDTH
    o = (acc_ref[0, 0:w] / acc_ref[0, w:w + 1]
         - lam * (acc_ref[1, 0:w] / acc_ref[1, w:w + 1]))
    o = o * lax.rsqrt(jnp.mean(o * o, axis=0, keepdims=True) + NORM_EPS)
    o_ref[0] = (o.T * sg_ref[...] * (1.0 - lambda_init)).astype(o_ref.dtype)


def _diff_attention(q, k, vt, slopes, lam_params, subln_g, lambda_init):
    b, s, d = q.shape
    n_kv = vt.shape[1]
    kern = functools.partial(_attn_kernel, lambda_init=lambda_init)
    qo_spec = pl.BlockSpec((1, Q_TILE, HEAD_WIDTH), lambda bi, hi, qi: (bi, qi, hi))
    return pl.pallas_call(
        kern,
        grid=(b, N_HEADS, s // Q_TILE),
        in_specs=[
            pl.BlockSpec(memory_space=pltpu.SMEM),
            qo_spec,
            pl.BlockSpec((1, s, HEAD_WIDTH), lambda bi, hi, qi: (bi, 0, hi)),
            pl.BlockSpec((1, n_kv, HEAD_WIDTH, KV_TILE), lambda bi, hi, qi: (bi, 0, hi, 0)),
            pl.BlockSpec(lam_params.shape, lambda bi, hi, qi: (0, 0)),
            pl.BlockSpec(subln_g.shape, lambda bi, hi, qi: (0, 0)),
        ],
        out_specs=qo_spec,
        out_shape=jax.ShapeDtypeStruct((b, s, d), _BF16),
        scratch_shapes=[pltpu.VMEM((2, KV_TILE, Q_TILE), _F32),
                        pltpu.VMEM((2, KV_TILE, Q_TILE), _F32),
                        pltpu.VMEM((2, HEAD_WIDTH + SUM_ROWS, Q_TILE), _F32)],
        compiler_params=pltpu.CompilerParams(
            dimension_semantics=("arbitrary", "arbitrary", "arbitrary"),
            vmem_limit_bytes=VMEM_LIMIT_BYTES),
        name="diff_attention",
    )(slopes, q, k, vt, lam_params, subln_g)


def _oproj_kernel(h_ref, o_ref, g_ref, wo_ref, out_ref):
    h = h_ref[...]
    out_ref[...] = h + _rms(_dot(o_ref[...], wo_ref[...]), g_ref[...])


def _o_proj(h, o, g, w_o):
    t, d = h.shape
    row = pl.BlockSpec((ROW_TILE, d), lambda i: (i, 0))
    return pl.pallas_call(
        _oproj_kernel,
        grid=(t // ROW_TILE,),
        in_specs=[row, row, _resident((1, d)), _resident(w_o.shape)],
        out_specs=row,
        out_shape=jax.ShapeDtypeStruct(h.shape, _F32),
        compiler_params=_row_params(("parallel",)),
        name="o_proj",
    )(h, o, g, w_o)


def kernel(x, ffn_w_gu, ffn_w_down, norm_g, conv_w_in, conv_k, conv_w_out, kv_norm_g, w_kv,
           attn_w_q, attn_lambda, attn_subln_g, attn_w_o):
    bsz, seq, d = x.shape
    assert d == D_MODEL and seq % ROW_TILE == 0 and seq % KV_TILE == 0 and Q_TILE == KV_TILE
    depth = ffn_w_gu.shape[0]
    assert depth == 2 * N_A_LAYERS

    bf = lambda w: w.astype(_BF16)
    g = lambda l, i: norm_g[l, i].reshape(1, d)
    h = x.reshape(bsz * seq, d)

    h = _ffn(h, g(0, 0), g(0, 1), bf(ffn_w_gu[0, 0]), bf(ffn_w_down[0, 0]))
    h = _conv_mixer(h, g(0, 2), g(0, 3), bf(conv_w_in[0]), conv_k[0], bf(conv_w_out[0]), seq)
    h = _ffn(h, g(0, 4), g(0, 5), bf(ffn_w_gu[0, 1]), bf(ffn_w_down[0, 1]))

    k, vt = _kv_proj(h, kv_norm_g.reshape(1, d), bf(w_kv[:, :d]), bf(w_kv[:, d:].T), seq)

    h = _ffn(h, g(1, 0), g(1, 1), bf(ffn_w_gu[1, 0]), bf(ffn_w_down[1, 0]))
    q = _q_proj(h, g(1, 2), bf(attn_w_q[0]))
    lambda_init = 0.8 - 0.6 * math.exp(-0.3 * 1)
    slopes = 2.0 ** (-8.0 * jnp.arange(1, N_HEADS + 1, dtype=_F32) / N_HEADS)
    o = _diff_attention(q.reshape(bsz, seq, d), k.reshape(bsz, seq, d), vt, slopes,
                        attn_lambda[0], attn_subln_g[0].reshape(1, HEAD_WIDTH), lambda_init)
    h = _o_proj(h, o.reshape(bsz * seq, d), g(1, 3), bf(attn_w_o[0]))
    h = _ffn(h, g(1, 4), g(1, 5), bf(ffn_w_gu[1, 1]), bf(ffn_w_down[1, 1]))
    return h.reshape(bsz, seq, d)
```

```python
import functools
import math

import jax
import jax.numpy as jnp
from jax import lax
from jax.experimental import pallas as pl
from jax.experimental.pallas import tpu as pltpu

D_MODEL = 1024
D_FF = 2816
HEAD_DIM = 64
HEAD_WIDTH = 2 * HEAD_DIM
N_HEADS = D_MODEL // HEAD_WIDTH
CONV_WIDTH = 3
NORM_EPS = 1e-6
N_A_LAYERS = 1

ROW_TILE = 512
FF_CHUNK = 256
CONV_CHUNK = 256
CONV_HALO = 8
Q_TILE = 512
KV_TILE = 512
SUM_ROWS = 16
LOG2_E = math.log2(math.e)
VMEM_LIMIT_BYTES = 56 * 1024 * 1024

_F32 = jnp.float32
_BF16 = jnp.bfloat16


def _rms(x, g):
    return x * lax.rsqrt(jnp.mean(x * x, axis=-1, keepdims=True) + NORM_EPS) * g


def _dot(a, b):
    return jnp.dot(a, b, preferred_element_type=_F32)


def _dot_nt(a, b):
    return lax.dot_general(a, b, (((1,), (1,)), ((), ())), preferred_element_type=_F32)


def _resident(shape):
    return pl.BlockSpec(shape, lambda *_: (0,) * len(shape), pipeline_mode=pl.Buffered(1))


def _row_params(semantics):
    return pltpu.CompilerParams(dimension_semantics=semantics, vmem_limit_bytes=VMEM_LIMIT_BYTES)


def _ffn_kernel(h_ref, gpre_ref, gpost_ref, wgu_ref, wd_ref, o_ref):
    h = h_ref[...]
    xn = _rms(h, gpre_ref[...]).astype(_BF16)
    acc = jnp.zeros(h.shape, _F32)
    for c in range(D_FF // FF_CHUNK):
        lo = c * FF_CHUNK
        gate = _dot(xn, wgu_ref[:, lo:lo + FF_CHUNK])
        up = _dot(xn, wgu_ref[:, D_FF + lo:D_FF + lo + FF_CHUNK])
        act = (jax.nn.silu(gate) * up).astype(_BF16)
        acc = acc + _dot(act, wd_ref[lo:lo + FF_CHUNK, :])
    o_ref[...] = h + 0.5 * _rms(acc, gpost_ref[...])


def _ffn(h, g_pre, g_post, w_gu, w_down):
    t, d = h.shape
    row = pl.BlockSpec((ROW_TILE, d), lambda i: (i, 0))
    return pl.pallas_call(
        _ffn_kernel,
        grid=(t // ROW_TILE,),
        in_specs=[row, _resident((1, d)), _resident((1, d)),
                  _resident(w_gu.shape), _resident(w_down.shape)],
        out_specs=row,
        out_shape=jax.ShapeDtypeStruct(h.shape, _F32),
        compiler_params=_row_params(("parallel",)),
        name="ffn_half",
    )(h, g_pre, g_post, w_gu, w_down)


def _conv_kernel(h_ref, gpre_ref, gpost_ref, win_ref, ck_ref, wout_ref, o_ref, u_ref, *, tiles_per_seq):
    tm, d = h_ref.shape
    first_of_seq = pl.program_id(0) % tiles_per_seq == 0

    @pl.when(first_of_seq)
    def _():
        u_ref[0:CONV_HALO, :] = jnp.zeros((CONV_HALO, d), _F32)

    @pl.when(jnp.logical_not(first_of_seq))
    def _():
        u_ref[0:CONV_HALO, :] = u_ref[tm:tm + CONV_HALO, :]

    h = h_ref[...]
    xn = _rms(h, gpre_ref[...]).astype(_BF16)
    acc = jnp.zeros(h.shape, _F32)
    for c in range(d // CONV_CHUNK):
        lo = c * CONV_CHUNK
        cols = slice(lo, lo + CONV_CHUNK)
        b_gate = _dot(xn, win_ref[:, lo:lo + CONV_CHUNK])
        c_gate = _dot(xn, win_ref[:, d + lo:d + lo + CONV_CHUNK])
        z = _dot(xn, win_ref[:, 2 * d + lo:2 * d + lo + CONV_CHUNK])
        u = c_gate * z
        u_ref[CONV_HALO:CONV_HALO + tm, cols] = u
        conv = (ck_ref[2:3, cols] * u
                + ck_ref[1:2, cols] * u_ref[CONV_HALO - 1:CONV_HALO - 1 + tm, cols]
                + ck_ref[0:1, cols] * u_ref[CONV_HALO - 2:CONV_HALO - 2 + tm, cols])
        gated = (b_gate * conv).astype(_BF16)
        acc = acc + _dot(gated, wout_ref[lo:lo + CONV_CHUNK, :])
    o_ref[...] = h + _rms(acc, gpost_ref[...])


def _conv_mixer(h, g_pre, g_post, w_in, conv_k, w_out, seq_len):
    t, d = h.shape
    row = pl.BlockSpec((ROW_TILE, d), lambda i: (i, 0))
    kern = functools.partial(_conv_kernel, tiles_per_seq=seq_len // ROW_TILE)
    return pl.pallas_call(
        kern,
        grid=(t // ROW_TILE,),
        in_specs=[row, _resident((1, d)), _resident((1, d)), _resident(w_in.shape),
                  _resident(conv_k.shape), _resident(w_out.shape)],
        out_specs=row,
        out_shape=jax.ShapeDtypeStruct(h.shape, _F32),
        scratch_shapes=[pltpu.VMEM((ROW_TILE + CONV_HALO, d), _F32)],
        compiler_params=_row_params(("arbitrary",)),
        name="conv_mixer",
    )(h, g_pre, g_post, w_in, conv_k, w_out)


def _kv_kernel(h_ref, g_ref, wk_ref, wvt_ref, k_ref, vt_ref):
    xn = _rms(h_ref[...], g_ref[...]).astype(_BF16)
    k_ref[...] = _dot(xn, wk_ref[...]).astype(_BF16)
    vt_ref[0, 0] = _dot_nt(wvt_ref[...], xn).astype(_BF16)


def _kv_proj(h, g, w_k, w_vt, seq_len):
    t, d = h.shape
    n_kv = seq_len // KV_TILE
    row = pl.BlockSpec((KV_TILE, d), lambda i: (i, 0))
    return pl.pallas_call(
        _kv_kernel,
        grid=(t // KV_TILE,),
        in_specs=[row, _resident((1, d)), _resident(w_k.shape), _resident(w_vt.shape)],
        out_specs=[row, pl.BlockSpec((1, 1, d, KV_TILE), lambda i: (i // n_kv, i % n_kv, 0, 0))],
        out_shape=[jax.ShapeDtypeStruct((t, d), _BF16),
                   jax.ShapeDtypeStruct((t // seq_len, n_kv, d, KV_TILE), _BF16)],
        compiler_params=_row_params(("parallel",)),
        name="kv_proj",
    )(h, g, w_k, w_vt)


def _q_kernel(h_ref, g_ref, wq_ref, q_ref):
    xn = _rms(h_ref[...], g_ref[...]).astype(_BF16)
    q_ref[...] = (_dot(xn, wq_ref[...]) * (HEAD_DIM ** -0.5 * LOG2_E)).astype(_BF16)


def _q_proj(h, g, w_q):
    t, d = h.shape
    row = pl.BlockSpec((ROW_TILE, d), lambda i: (i, 0))
    return pl.pallas_call(
        _q_kernel,
        grid=(t // ROW_TILE,),
        in_specs=[row, _resident((1, d)), _resident(w_q.shape)],
        out_specs=row,
        out_shape=jax.ShapeDtypeStruct((t, d), _BF16),
        compiler_params=_row_params(("parallel",)),
        name="q_proj",
    )(h, g, w_q)


def _attn_kernel(slopes_ref, q_ref, k_ref, vt_ref, lam_ref, sg_ref, o_ref,
                 qm_ref, bias_ref, s_ref, acc_ref, *, lambda_init, tq):
    seq = q_ref.shape[1]
    tk = vt_ref.shape[3]
    n_q = seq // tq
    slope = slopes_ref[pl.program_id(1)] * LOG2_E

    kk = lax.broadcasted_iota(jnp.int32, (tk, tq), 0)
    qq = lax.broadcasted_iota(jnp.int32, (tk, tq), 1)
    bias = slope * (kk - qq).astype(_F32)
    bias_ref[0] = bias
    bias_ref[1] = jnp.where(kk <= qq, bias, -jnp.inf)

    q = q_ref[0]
    lane = lax.broadcasted_iota(jnp.int32, q.shape, 1)
    zero = jnp.zeros_like(q)
    qm_ref[0] = jnp.where(lane < HEAD_DIM, q, zero)
    qm_ref[1] = jnp.where(lane >= HEAD_DIM, q, zero)

    ones_rows = jnp.ones((SUM_ROWS, tk), _BF16)
    acc_ref[...] = jnp.zeros(acc_ref.shape, _F32)

    lp = lam_ref[...]
    lam = (jnp.exp(jnp.sum(lp[0:1] * lp[1:2], axis=-1, keepdims=True))
           - jnp.exp(jnp.sum(lp[2:3] * lp[3:4], axis=-1, keepdims=True)) + lambda_init)

    def block_offset(qi, j):
        return slope * (j * tk - qi * tq).astype(_F32)

    def scores(qi, j):
        kb = k_ref[0, pl.ds(pl.multiple_of(j * tk, tk), tk), :]
        rows = pl.ds(pl.multiple_of(qi * tq, tq), tq)
        pair_bias = bias_ref[(j == qi).astype(jnp.int32)]
        s_pair = [_dot_nt(kb, qm_ref[c, rows, :]) + pair_bias for c in range(2)]
        tops = [jnp.max(s, axis=0, keepdims=True) + block_offset(qi, j) for s in s_pair]
        return s_pair, tops

    def exp_pv(qi, j, s_pair, tops, maxima):
        vb = jnp.concatenate([vt_ref[0, j], ones_rows], axis=0)
        new_maxima = []
        for c in range(2):
            m_old = jnp.where(j == 0, -jnp.inf, maxima[c])
            m_new = jnp.maximum(m_old, tops[c])
            alpha = jnp.exp2(m_old - m_new)
            p = jnp.exp2(s_pair[c] - (m_new - block_offset(qi, j)))
            acc_ref[c] = alpha * acc_ref[c] + _dot(vb, p.astype(_BF16))
            new_maxima.append(m_new)
        return new_maxima

    def finish(qi):
        w = HEAD_WIDTH
        o = (acc_ref[0, 0:w] / acc_ref[0, w:w + 1]
             - lam * (acc_ref[1, 0:w] / acc_ref[1, w:w + 1]))
        o = o * lax.rsqrt(jnp.mean(o * o, axis=0, keepdims=True) + NORM_EPS)
        rows = pl.ds(pl.multiple_of(qi * tq, tq), tq)
        o_ref[0, rows, :] = (o.T * sg_ref[...] * (1.0 - lambda_init)).astype(o_ref.dtype)

    first, first_tops = scores(jnp.int32(0), jnp.int32(0))
    for c in range(2):
        s_ref[c] = first[c]

    def body(_, carry):
        qi, j = carry[0], carry[1]
        maxima, tops = carry[2:4], carry[4:6]
        last = j == qi
        qi_next = jnp.where(last, qi + 1, qi)
        j_next = jnp.where(last, 0, j + 1)
        cur = [s_ref[0], s_ref[1]]
        nxt, nxt_tops = scores(qi_next, j_next)
        maxima = exp_pv(qi, j, cur, tops, maxima)
        for c in range(2):
            s_ref[c] = nxt[c]
        pl.when(last)(lambda: finish(qi))
        return (qi_next, j_next, *maxima, *nxt_tops)

    neg_inf = jnp.full((1, tq), -jnp.inf, _F32)
    n_pairs = n_q * (n_q + 1) // 2
    carry = lax.fori_loop(0, n_pairs - 1, body,
                          (jnp.int32(0), jnp.int32(0), neg_inf, neg_inf, *first_tops))
    exp_pv(carry[0], carry[1], [s_ref[0], s_ref[1]], carry[4:6], carry[2:4])
    finish(carry[0])


def _diff_attention(q, k, vt, slopes, lam_params, subln_g, lambda_init):
    b, s, d = q.shape
    n_kv = vt.shape[1]
    kern = functools.partial(_attn_kernel, lambda_init=lambda_init, tq=Q_TILE)
    head_rows = pl.BlockSpec((1, s, HEAD_WIDTH), lambda bi, hi: (bi, 0, hi))
    return pl.pallas_call(
        kern,
        grid=(b, N_HEADS),
        in_specs=[
            pl.BlockSpec(memory_space=pltpu.SMEM),
            head_rows,
            head_rows,
            pl.BlockSpec((1, n_kv, HEAD_WIDTH, KV_TILE), lambda bi, hi: (bi, 0, hi, 0)),
            pl.BlockSpec(lam_params.shape, lambda bi, hi: (0, 0)),
            pl.BlockSpec(subln_g.shape, lambda bi, hi: (0, 0)),
        ],
        out_specs=head_rows,
        out_shape=jax.ShapeDtypeStruct((b, s, d), _BF16),
        scratch_shapes=[pltpu.VMEM((2, s, HEAD_WIDTH), _BF16),
                        pltpu.VMEM((2, KV_TILE, Q_TILE), _F32),
                        pltpu.VMEM((2, KV_TILE, Q_TILE), _F32),
                        pltpu.VMEM((2, HEAD_WIDTH + SUM_ROWS, Q_TILE), _F32)],
        compiler_params=pltpu.CompilerParams(
            dimension_semantics=("parallel", "parallel"),
            vmem_limit_bytes=VMEM_LIMIT_BYTES),
        name="diff_attention",
    )(slopes, q, k, vt, lam_params, subln_g)


def _oproj_kernel(h_ref, o_ref, g_ref, wo_ref, out_ref):
    h = h_ref[...]
    out_ref[...] = h + _rms(_dot(o_ref[...], wo_ref[...]), g_ref[...])


def _o_proj(h, o, g, w_o):
    t, d = h.shape
    row = pl.BlockSpec((ROW_TILE, d), lambda i: (i, 0))
    return pl.pallas_call(
        _oproj_kernel,
        grid=(t // ROW_TILE,),
        in_specs=[row, row, _resident((1, d)), _resident(w_o.shape)],
        out_specs=row,
        out_shape=jax.ShapeDtypeStruct(h.shape, _F32),
        compiler_params=_row_params(("parallel",)),
        name="o_proj",
    )(h, o, g, w_o)


def kernel(x, ffn_w_gu, ffn_w_down, norm_g, conv_w_in, conv_k, conv_w_out, kv_norm_g, w_kv,
           attn_w_q, attn_lambda, attn_subln_g, attn_w_o):
    bsz, seq, d = x.shape
    assert d == D_MODEL and seq % ROW_TILE == 0 and seq % KV_TILE == 0 and Q_TILE == KV_TILE
    depth = ffn_w_gu.shape[0]
    assert depth == 2 * N_A_LAYERS

    bf = lambda w: w.astype(_BF16)
    g = lambda l, i: norm_g[l, i].reshape(1, d)
    h = x.reshape(bsz * seq, d)

    h = _ffn(h, g(0, 0), g(0, 1), bf(ffn_w_gu[0, 0]), bf(ffn_w_down[0, 0]))
    h = _conv_mixer(h, g(0, 2), g(0, 3), bf(conv_w_in[0]), conv_k[0], bf(conv_w_out[0]), seq)
    h = _ffn(h, g(0, 4), g(0, 5), bf(ffn_w_gu[0, 1]), bf(ffn_w_down[0, 1]))

    k, vt = _kv_proj(h, kv_norm_g.reshape(1, d), bf(w_kv[:, :d]), bf(w_kv[:, d:].T), seq)

    h = _ffn(h, g(1, 0), g(1, 1), bf(ffn_w_gu[1, 0]), bf(ffn_w_down[1, 0]))
    q = _q_proj(h, g(1, 2), bf(attn_w_q[0]))
    lambda_init = 0.8 - 0.6 * math.exp(-0.3 * 1)
    slopes = 2.0 ** (-8.0 * jnp.arange(1, N_HEADS + 1, dtype=_F32) / N_HEADS)
    o = _diff_attention(q.reshape(bsz, seq, d), k.reshape(bsz, seq, d), vt, slopes,
                        attn_lambda[0], attn_subln_g[0].reshape(1, HEAD_WIDTH), lambda_init)
    h = _o_proj(h, o.reshape(bsz * seq, d), g(1, 3), bf(attn_w_o[0]))
    h = _ffn(h, g(1, 4), g(1, 5), bf(ffn_w_gu[1, 1]), bf(ffn_w_down[1, 1]))
    return h.reshape(bsz, seq, d)
```

```python
import functools
import math

import jax
import jax.numpy as jnp
from jax import lax
from jax.experimental import pallas as pl
from jax.experimental.pallas import tpu as pltpu

D_MODEL = 1024
D_FF = 2816
HEAD_DIM = 64
HEAD_WIDTH = 2 * HEAD_DIM
N_HEADS = D_MODEL // HEAD_WIDTH
CONV_WIDTH = 3
NORM_EPS = 1e-6
N_A_LAYERS = 1

ROW_TILE = 512
FF_CHUNK = 256
CONV_CHUNK = 256
CONV_HALO = 8
Q_TILE = 512
KV_TILE = 512
PAIRS_PER_TRIP = 4
SUM_ROWS = 16
LOG2_E = math.log2(math.e)
VMEM_LIMIT_BYTES = 56 * 1024 * 1024

_F32 = jnp.float32
_BF16 = jnp.bfloat16


def _rms(x, g):
    return x * lax.rsqrt(jnp.mean(x * x, axis=-1, keepdims=True) + NORM_EPS) * g


def _dot(a, b):
    return jnp.dot(a, b, preferred_element_type=_F32)


def _dot_nt(a, b):
    return lax.dot_general(a, b, (((1,), (1,)), ((), ())), preferred_element_type=_F32)


def _resident(shape):
    return pl.BlockSpec(shape, lambda *_: (0,) * len(shape), pipeline_mode=pl.Buffered(1))


def _row_params(semantics):
    return pltpu.CompilerParams(dimension_semantics=semantics, vmem_limit_bytes=VMEM_LIMIT_BYTES)


def _ffn_kernel(h_ref, gpre_ref, gpost_ref, wgu_ref, wd_ref, o_ref):
    h = h_ref[...]
    xn = _rms(h, gpre_ref[...]).astype(_BF16)
    acc = jnp.zeros(h.shape, _F32)
    for c in range(D_FF // FF_CHUNK):
        lo = c * FF_CHUNK
        gate = _dot(xn, wgu_ref[:, lo:lo + FF_CHUNK])
        up = _dot(xn, wgu_ref[:, D_FF + lo:D_FF + lo + FF_CHUNK])
        act = (jax.nn.silu(gate) * up).astype(_BF16)
        acc = acc + _dot(act, wd_ref[lo:lo + FF_CHUNK, :])
    o_ref[...] = h + 0.5 * _rms(acc, gpost_ref[...])


def _ffn(h, g_pre, g_post, w_gu, w_down):
    t, d = h.shape
    row = pl.BlockSpec((ROW_TILE, d), lambda i: (i, 0))
    return pl.pallas_call(
        _ffn_kernel,
        grid=(t // ROW_TILE,),
        in_specs=[row, _resident((1, d)), _resident((1, d)),
                  _resident(w_gu.shape), _resident(w_down.shape)],
        out_specs=row,
        out_shape=jax.ShapeDtypeStruct(h.shape, _F32),
        compiler_params=_row_params(("parallel",)),
        name="ffn_half",
    )(h, g_pre, g_post, w_gu, w_down)


def _conv_kernel(h_ref, gpre_ref, gpost_ref, win_ref, ck_ref, wout_ref, o_ref, u_ref, *, tiles_per_seq):
    tm, d = h_ref.shape
    first_of_seq = pl.program_id(0) % tiles_per_seq == 0

    @pl.when(first_of_seq)
    def _():
        u_ref[0:CONV_HALO, :] = jnp.zeros((CONV_HALO, d), _F32)

    @pl.when(jnp.logical_not(first_of_seq))
    def _():
        u_ref[0:CONV_HALO, :] = u_ref[tm:tm + CONV_HALO, :]

    h = h_ref[...]
    xn = _rms(h, gpre_ref[...]).astype(_BF16)
    acc = jnp.zeros(h.shape, _F32)
    for c in range(d // CONV_CHUNK):
        lo = c * CONV_CHUNK
        cols = slice(lo, lo + CONV_CHUNK)
        b_gate = _dot(xn, win_ref[:, lo:lo + CONV_CHUNK])
        c_gate = _dot(xn, win_ref[:, d + lo:d + lo + CONV_CHUNK])
        z = _dot(xn, win_ref[:, 2 * d + lo:2 * d + lo + CONV_CHUNK])
        u = c_gate * z
        u_ref[CONV_HALO:CONV_HALO + tm, cols] = u
        conv = (ck_ref[2:3, cols] * u
                + ck_ref[1:2, cols] * u_ref[CONV_HALO - 1:CONV_HALO - 1 + tm, cols]
                + ck_ref[0:1, cols] * u_ref[CONV_HALO - 2:CONV_HALO - 2 + tm, cols])
        gated = (b_gate * conv).astype(_BF16)
        acc = acc + _dot(gated, wout_ref[lo:lo + CONV_CHUNK, :])
    o_ref[...] = h + _rms(acc, gpost_ref[...])


def _conv_mixer(h, g_pre, g_post, w_in, conv_k, w_out, seq_len):
    t, d = h.shape
    row = pl.BlockSpec((ROW_TILE, d), lambda i: (i, 0))
    kern = functools.partial(_conv_kernel, tiles_per_seq=seq_len // ROW_TILE)
    return pl.pallas_call(
        kern,
        grid=(t // ROW_TILE,),
        in_specs=[row, _resident((1, d)), _resident((1, d)), _resident(w_in.shape),
                  _resident(conv_k.shape), _resident(w_out.shape)],
        out_specs=row,
        out_shape=jax.ShapeDtypeStruct(h.shape, _F32),
        scratch_shapes=[pltpu.VMEM((ROW_TILE + CONV_HALO, d), _F32)],
        compiler_params=_row_params(("arbitrary",)),
        name="conv_mixer",
    )(h, g_pre, g_post, w_in, conv_k, w_out)


def _kv_kernel(h_ref, g_ref, wk_ref, wvt_ref, k_ref, vt_ref):
    xn = _rms(h_ref[...], g_ref[...]).astype(_BF16)
    k_ref[...] = _dot(xn, wk_ref[...]).astype(_BF16)
    vt_ref[0, 0] = _dot_nt(wvt_ref[...], xn).astype(_BF16)


def _kv_proj(h, g, w_k, w_vt, seq_len):
    t, d = h.shape
    n_kv = seq_len // KV_TILE
    row = pl.BlockSpec((KV_TILE, d), lambda i: (i, 0))
    return pl.pallas_call(
        _kv_kernel,
        grid=(t // KV_TILE,),
        in_specs=[row, _resident((1, d)), _resident(w_k.shape), _resident(w_vt.shape)],
        out_specs=[row, pl.BlockSpec((1, 1, d, KV_TILE), lambda i: (i // n_kv, i % n_kv, 0, 0))],
        out_shape=[jax.ShapeDtypeStruct((t, d), _BF16),
                   jax.ShapeDtypeStruct((t // seq_len, n_kv, d, KV_TILE), _BF16)],
        compiler_params=_row_params(("parallel",)),
        name="kv_proj",
    )(h, g, w_k, w_vt)


def _q_kernel(h_ref, g_ref, wq_ref, q_ref):
    xn = _rms(h_ref[...], g_ref[...]).astype(_BF16)
    q_ref[...] = (_dot(xn, wq_ref[...]) * (HEAD_DIM ** -0.5 * LOG2_E)).astype(_BF16)


def _q_proj(h, g, w_q):
    t, d = h.shape
    row = pl.BlockSpec((ROW_TILE, d), lambda i: (i, 0))
    return pl.pallas_call(
        _q_kernel,
        grid=(t // ROW_TILE,),
        in_specs=[row, _resident((1, d)), _resident(w_q.shape)],
        out_specs=row,
        out_shape=jax.ShapeDtypeStruct((t, d), _BF16),
        compiler_params=_row_params(("parallel",)),
        name="q_proj",
    )(h, g, w_q)


def _attn_kernel(slopes_ref, q_ref, k_ref, vt_ref, lam_ref, sg_ref, o_ref,
                 qm_ref, bias_ref, s_ref, acc_ref, fin_ref, *, lambda_init, tq):
    seq = q_ref.shape[1]
    tk = vt_ref.shape[3]
    n_q = seq // tq
    slope = slopes_ref[pl.program_id(1)] * LOG2_E

    kk = lax.broadcasted_iota(jnp.int32, (tk, tq), 0)
    qq = lax.broadcasted_iota(jnp.int32, (tk, tq), 1)
    bias = slope * (kk - qq).astype(_F32)
    bias_ref[0] = bias
    bias_ref[1] = jnp.where(kk <= qq, bias, -jnp.inf)

    q = q_ref[0]
    lane = lax.broadcasted_iota(jnp.int32, q.shape, 1)
    zero = jnp.zeros_like(q)
    qm_ref[0] = jnp.where(lane < HEAD_DIM, q, zero)
    qm_ref[1] = jnp.where(lane >= HEAD_DIM, q, zero)

    ones_rows = jnp.ones((SUM_ROWS, tk), _BF16)
    acc_ref[...] = jnp.zeros(acc_ref.shape, _F32)

    lp = lam_ref[...]
    lam = (jnp.exp(jnp.sum(lp[0:1] * lp[1:2], axis=-1, keepdims=True))
           - jnp.exp(jnp.sum(lp[2:3] * lp[3:4], axis=-1, keepdims=True)) + lambda_init)

    def block_offset(qi, j):
        return slope * (j * tk - qi * tq).astype(_F32)

    def scores(slot, qi, j):
        kb = k_ref[0, pl.ds(pl.multiple_of(j * tk, tk), tk), :]
        rows = pl.ds(pl.multiple_of(qi * tq, tq), tq)
        pair_bias = bias_ref[(j == qi).astype(jnp.int32)]
        tops = []
        for c in range(2):
            s = _dot_nt(kb, qm_ref[c, rows, :]) + pair_bias
            s_ref[slot, c] = s
            tops.append(jnp.max(s, axis=0, keepdims=True) + block_offset(qi, j))
        return tops

    def exp_pv(slot, step, qi, j, tops, maxima):
        vb = jnp.concatenate([vt_ref[0, j], ones_rows], axis=0)
        new_maxima = []
        for c in range(2):
            m_old = jnp.where(j == 0, -jnp.inf, maxima[c])
            m_new = jnp.maximum(m_old, tops[c])
            alpha = jnp.exp2(m_old - m_new)
            p = jnp.exp2(s_ref[slot, c] - (m_new - block_offset(qi, j)))
            acc = alpha * acc_ref[c] + _dot(vb, p.astype(_BF16))
            acc_ref[c] = acc
            fin_ref[step, c] = acc
            new_maxima.append(m_new)
        return new_maxima

    def finish(step, qi):
        w = HEAD_WIDTH
        o = (fin_ref[step, 0, 0:w] / fin_ref[step, 0, w:w + 1]
             - lam * (fin_ref[step, 1, 0:w] / fin_ref[step, 1, w:w + 1]))
        o = o * lax.rsqrt(jnp.mean(o * o, axis=0, keepdims=True) + NORM_EPS)
        rows = pl.ds(pl.multiple_of(qi * tq, tq), tq)
        o_ref[0, rows, :] = (o.T * sg_ref[...] * (1.0 - lambda_init)).astype(o_ref.dtype)

    def advance(qi, j):
        last = j == qi
        return jnp.where(last, qi + 1, qi), jnp.where(last, 0, j + 1)

    def trip(carry, final):
        pairs = [carry[0:2]]
        for _ in range(PAIRS_PER_TRIP):
            pairs.append(advance(*pairs[-1]))
        maxima, tops = carry[2:4], carry[4:6]
        for step in range(PAIRS_PER_TRIP):
            last_of_kernel = final and step == PAIRS_PER_TRIP - 1
            if not last_of_kernel:
                next_tops = scores((step + 1) % 2, *pairs[step + 1])
            maxima = exp_pv(step % 2, step, *pairs[step], tops, maxima)
            tops = None if last_of_kernel else next_tops
        for step in range(PAIRS_PER_TRIP):
            qi, j = pairs[step]
            if final and step == PAIRS_PER_TRIP - 1:
                finish(step, qi)
            else:
                pl.when(j == qi)(functools.partial(finish, step, qi))
        return None if final else (*pairs[PAIRS_PER_TRIP], *maxima, *tops)

    neg_inf = jnp.full((1, tq), -jnp.inf, _F32)
    n_pairs = n_q * (n_q + 1) // 2
    assert n_pairs % PAIRS_PER_TRIP == 0 and PAIRS_PER_TRIP % 2 == 0
    first = (jnp.int32(0), jnp.int32(0))
    carry = lax.fori_loop(0, n_pairs // PAIRS_PER_TRIP - 1, lambda _, c: trip(c, final=False),
                          (*first, neg_inf, neg_inf, *scores(0, *first)))
    trip(carry, final=True)


def _diff_attention(q, k, vt, slopes, lam_params, subln_g, lambda_init):
    b, s, d = q.shape
    n_kv = vt.shape[1]
    kern = functools.partial(_attn_kernel, lambda_init=lambda_init, tq=Q_TILE)
    head_rows = pl.BlockSpec((1, s, HEAD_WIDTH), lambda bi, hi: (bi, 0, hi))
    return pl.pallas_call(
        kern,
        grid=(b, N_HEADS),
        in_specs=[
            pl.BlockSpec(memory_space=pltpu.SMEM),
            head_rows,
            head_rows,
            pl.BlockSpec((1, n_kv, HEAD_WIDTH, KV_TILE), lambda bi, hi: (bi, 0, hi, 0)),
            pl.BlockSpec(lam_params.shape, lambda bi, hi: (0, 0)),
            pl.BlockSpec(subln_g.shape, lambda bi, hi: (0, 0)),
        ],
        out_specs=head_rows,
        out_shape=jax.ShapeDtypeStruct((b, s, d), _BF16),
        scratch_shapes=[pltpu.VMEM((2, s, HEAD_WIDTH), _BF16),
                        pltpu.VMEM((2, KV_TILE, Q_TILE), _F32),
                        pltpu.VMEM((2, 2, KV_TILE, Q_TILE), _F32),
                        pltpu.VMEM((2, HEAD_WIDTH + SUM_ROWS, Q_TILE), _F32),
                        pltpu.VMEM((PAIRS_PER_TRIP, 2, HEAD_WIDTH + SUM_ROWS, Q_TILE), _F32)],
        compiler_params=pltpu.CompilerParams(
            dimension_semantics=("parallel", "parallel"),
            vmem_limit_bytes=VMEM_LIMIT_BYTES),
        name="diff_attention",
    )(slopes, q, k, vt, lam_params, subln_g)


def _oproj_kernel(h_ref, o_ref, g_ref, wo_ref, out_ref):
    h = h_ref[...]
    out_ref[...] = h + _rms(_dot(o_ref[...], wo_ref[...]), g_ref[...])


def _o_proj(h, o, g, w_o):
    t, d = h.shape
    row = pl.BlockSpec((ROW_TILE, d), lambda i: (i, 0))
    return pl.pallas_call(
        _oproj_kernel,
        grid=(t // ROW_TILE,),
        in_specs=[row, row, _resident((1, d)), _resident(w_o.shape)],
        out_specs=row,
        out_shape=jax.ShapeDtypeStruct(h.shape, _F32),
        compiler_params=_row_params(("parallel",)),
        name="o_proj",
    )(h, o, g, w_o)


def kernel(x, ffn_w_gu, ffn_w_down, norm_g, conv_w_in, conv_k, conv_w_out, kv_norm_g, w_kv,
           attn_w_q, attn_lambda, attn_subln_g, attn_w_o):
    bsz, seq, d = x.shape
    assert d == D_MODEL and seq % ROW_TILE == 0 and seq % KV_TILE == 0 and Q_TILE == KV_TILE
    depth = ffn_w_gu.shape[0]
    assert depth == 2 * N_A_LAYERS

    bf = lambda w: w.astype(_BF16)
    g = lambda l, i: norm_g[l, i].reshape(1, d)
    h = x.reshape(bsz * seq, d)

    h = _ffn(h, g(0, 0), g(0, 1), bf(ffn_w_gu[0, 0]), bf(ffn_w_down[0, 0]))
    h = _conv_mixer(h, g(0, 2), g(0, 3), bf(conv_w_in[0]), conv_k[0], bf(conv_w_out[0]), seq)
    h = _ffn(h, g(0, 4), g(0, 5), bf(ffn_w_gu[0, 1]), bf(ffn_w_down[0, 1]))

    k, vt = _kv_proj(h, kv_norm_g.reshape(1, d), bf(w_kv[:, :d]), bf(w_kv[:, d:].T), seq)

    h = _ffn(h, g(1, 0), g(1, 1), bf(ffn_w_gu[1, 0]), bf(ffn_w_down[1, 0]))
    q = _q_proj(h, g(1, 2), bf(attn_w_q[0]))
    lambda_init = 0.8 - 0.6 * math.exp(-0.3 * 1)
    slopes = 2.0 ** (-8.0 * jnp.arange(1, N_HEADS + 1, dtype=_F32) / N_HEADS)
    o = _diff_attention(q.reshape(bsz, seq, d), k.reshape(bsz, seq, d), vt, slopes,
                        attn_lambda[0], attn_subln_g[0].reshape(1, HEAD_WIDTH), lambda_init)
    h = _o_proj(h, o.reshape(bsz * seq, d), g(1, 3), bf(attn_w_o[0]))
    h = _ffn(h, g(1, 4), g(1, 5), bf(ffn_w_gu[1, 1]), bf(ffn_w_down[1, 1]))
    return h.reshape(bsz, seq, d)
```

```python
import functools
import math

import jax
import jax.numpy as jnp
from jax import lax
from jax.experimental import pallas as pl
from jax.experimental.pallas import tpu as pltpu

D_MODEL = 1024
D_FF = 2816
HEAD_DIM = 64
HEAD_WIDTH = 2 * HEAD_DIM
N_HEADS = D_MODEL // HEAD_WIDTH
CONV_WIDTH = 3
NORM_EPS = 1e-6
N_A_LAYERS = 1

ROW_TILE = 1024
SUB_ROWS = 512
FF_CHUNK = 256
CONV_CHUNK = 256
CONV_HALO = 8
Q_TILE = 512
KV_TILE = 512
PAIRS_PER_TRIP = 4
SUM_ROWS = 16
LOG2_E = math.log2(math.e)
VMEM_LIMIT_BYTES = 56 * 1024 * 1024

_F32 = jnp.float32
_BF16 = jnp.bfloat16


def _rms(x, g):
    return x * lax.rsqrt(jnp.mean(x * x, axis=-1, keepdims=True) + NORM_EPS) * g


def _dot(a, b):
    return jnp.dot(a, b, preferred_element_type=_F32)


def _dot_nt(a, b):
    return lax.dot_general(a, b, (((1,), (1,)), ((), ())), preferred_element_type=_F32)


def _resident(shape):
    return pl.BlockSpec(shape, lambda *_: (0,) * len(shape), pipeline_mode=pl.Buffered(1))


def _row_params(semantics):
    return pltpu.CompilerParams(dimension_semantics=semantics, vmem_limit_bytes=VMEM_LIMIT_BYTES)


def _ffn_kernel(h_ref, gpre_ref, gpost_ref, wgu_ref, wd_ref, o_ref):
    for r in range(h_ref.shape[0] // SUB_ROWS):
        rows = slice(r * SUB_ROWS, (r + 1) * SUB_ROWS)
        h = h_ref[rows, :]
        xn = _rms(h, gpre_ref[...]).astype(_BF16)
        acc = jnp.zeros(h.shape, _F32)
        for c in range(D_FF // FF_CHUNK):
            lo = c * FF_CHUNK
            gate = _dot(xn, wgu_ref[:, lo:lo + FF_CHUNK])
            up = _dot(xn, wgu_ref[:, D_FF + lo:D_FF + lo + FF_CHUNK])
            act = (jax.nn.silu(gate) * up).astype(_BF16)
            acc = acc + _dot(act, wd_ref[lo:lo + FF_CHUNK, :])
        o_ref[rows, :] = h + 0.5 * _rms(acc, gpost_ref[...])


def _ffn(h, g_pre, g_post, w_gu, w_down, layer, half):
    t, d = h.shape
    row = pl.BlockSpec((ROW_TILE, d), lambda i: (i, 0))

    def picked(w):
        return pl.BlockSpec((None, None) + w.shape[2:], lambda i: (layer, half, 0, 0),
                            pipeline_mode=pl.Buffered(1))

    return pl.pallas_call(
        _ffn_kernel,
        grid=(t // ROW_TILE,),
        in_specs=[row, _resident((1, d)), _resident((1, d)), picked(w_gu), picked(w_down)],
        out_specs=row,
        out_shape=jax.ShapeDtypeStruct(h.shape, _F32),
        compiler_params=_row_params(("parallel",)),
        name="ffn_half",
    )(h, g_pre, g_post, w_gu, w_down)


def _conv_kernel(h_ref, gpre_ref, gpost_ref, win_ref, ck_ref, wout_ref, o_ref, u_ref, *, tiles_per_seq):
    tm, d = h_ref.shape
    first_of_seq = pl.program_id(0) % tiles_per_seq == 0

    @pl.when(first_of_seq)
    def _():
        u_ref[0:CONV_HALO, :] = jnp.zeros((CONV_HALO, d), _F32)

    @pl.when(jnp.logical_not(first_of_seq))
    def _():
        u_ref[0:CONV_HALO, :] = u_ref[tm:tm + CONV_HALO, :]

    h = h_ref[...]
    xn = _rms(h, gpre_ref[...]).astype(_BF16)
    acc = jnp.zeros(h.shape, _F32)
    for c in range(d // CONV_CHUNK):
        lo = c * CONV_CHUNK
        cols = slice(lo, lo + CONV_CHUNK)
        b_gate = _dot(xn, win_ref[:, lo:lo + CONV_CHUNK])
        c_gate = _dot(xn, win_ref[:, d + lo:d + lo + CONV_CHUNK])
        z = _dot(xn, win_ref[:, 2 * d + lo:2 * d + lo + CONV_CHUNK])
        u = c_gate * z
        u_ref[CONV_HALO:CONV_HALO + tm, cols] = u
        conv = (ck_ref[2:3, cols] * u
                + ck_ref[1:2, cols] * u_ref[CONV_HALO - 1:CONV_HALO - 1 + tm, cols]
                + ck_ref[0:1, cols] * u_ref[CONV_HALO - 2:CONV_HALO - 2 + tm, cols])
        gated = (b_gate * conv).astype(_BF16)
        acc = acc + _dot(gated, wout_ref[lo:lo + CONV_CHUNK, :])
    o_ref[...] = h + _rms(acc, gpost_ref[...])


def _conv_mixer(h, g_pre, g_post, w_in, conv_k, w_out, seq_len):
    t, d = h.shape
    row = pl.BlockSpec((ROW_TILE, d), lambda i: (i, 0))
    kern = functools.partial(_conv_kernel, tiles_per_seq=seq_len // ROW_TILE)
    return pl.pallas_call(
        kern,
        grid=(t // ROW_TILE,),
        in_specs=[row, _resident((1, d)), _resident((1, d)), _resident(w_in.shape),
                  _resident(conv_k.shape), _resident(w_out.shape)],
        out_specs=row,
        out_shape=jax.ShapeDtypeStruct(h.shape, _F32),
        scratch_shapes=[pltpu.VMEM((ROW_TILE + CONV_HALO, d), _F32)],
        compiler_params=_row_params(("arbitrary",)),
        name="conv_mixer",
    )(h, g_pre, g_post, w_in, conv_k, w_out)


def _kv_kernel(h_ref, g_ref, wk_ref, wvt_ref, k_ref, vt_ref):
    xn = _rms(h_ref[...], g_ref[...]).astype(_BF16)
    k_ref[...] = _dot(xn, wk_ref[...]).astype(_BF16)
    vt_ref[0, 0] = _dot_nt(wvt_ref[...], xn).astype(_BF16)


def _kv_proj(h, g, w_k, w_vt, seq_len):
    t, d = h.shape
    n_kv = seq_len // KV_TILE
    row = pl.BlockSpec((KV_TILE, d), lambda i: (i, 0))
    return pl.pallas_call(
        _kv_kernel,
        grid=(t // KV_TILE,),
        in_specs=[row, _resident((1, d)), _resident(w_k.shape), _resident(w_vt.shape)],
        out_specs=[row, pl.BlockSpec((1, 1, d, KV_TILE), lambda i: (i // n_kv, i % n_kv, 0, 0))],
        out_shape=[jax.ShapeDtypeStruct((t, d), _BF16),
                   jax.ShapeDtypeStruct((t // seq_len, n_kv, d, KV_TILE), _BF16)],
        compiler_params=_row_params(("parallel",)),
        name="kv_proj",
    )(h, g, w_k, w_vt)


def _q_kernel(h_ref, g_ref, wq_ref, q_ref):
    xn = _rms(h_ref[...], g_ref[...]).astype(_BF16)
    q_ref[...] = (_dot(xn, wq_ref[...]) * (HEAD_DIM ** -0.5 * LOG2_E)).astype(_BF16)


def _q_proj(h, g, w_q):
    t, d = h.shape
    row = pl.BlockSpec((ROW_TILE, d), lambda i: (i, 0))
    return pl.pallas_call(
        _q_kernel,
        grid=(t // ROW_TILE,),
        in_specs=[row, _resident((1, d)), _resident(w_q.shape)],
        out_specs=row,
        out_shape=jax.ShapeDtypeStruct((t, d), _BF16),
        compiler_params=_row_params(("parallel",)),
        name="q_proj",
    )(h, g, w_q)


def _attn_kernel(slopes_ref, q_ref, k_ref, vt_ref, lam_ref, sg_ref, o_ref,
                 qm_ref, bias_ref, s_ref, acc_ref, fin_ref, *, lambda_init, tq):
    seq = q_ref.shape[1]
    tk = vt_ref.shape[3]
    n_q = seq // tq
    slope = slopes_ref[pl.program_id(1)] * LOG2_E

    kk = lax.broadcasted_iota(jnp.int32, (tk, tq), 0)
    qq = lax.broadcasted_iota(jnp.int32, (tk, tq), 1)
    bias = slope * (kk - qq).astype(_F32)
    bias_ref[0] = bias
    bias_ref[1] = jnp.where(kk <= qq, bias, -jnp.inf)

    q = q_ref[0]
    lane = lax.broadcasted_iota(jnp.int32, q.shape, 1)
    zero = jnp.zeros_like(q)
    qm_ref[0] = jnp.where(lane < HEAD_DIM, q, zero)
    qm_ref[1] = jnp.where(lane >= HEAD_DIM, q, zero)

    ones_rows = jnp.ones((SUM_ROWS, tk), _BF16)
    acc_ref[...] = jnp.zeros(acc_ref.shape, _F32)

    lp = lam_ref[...]
    lam = (jnp.exp(jnp.sum(lp[0:1] * lp[1:2], axis=-1, keepdims=True))
           - jnp.exp(jnp.sum(lp[2:3] * lp[3:4], axis=-1, keepdims=True)) + lambda_init)

    def block_offset(qi, j):
        return slope * (j * tk - qi * tq).astype(_F32)

    def scores(slot, qi, j):
        kb = k_ref[0, pl.ds(pl.multiple_of(j * tk, tk), tk), :]
        rows = pl.ds(pl.multiple_of(qi * tq, tq), tq)
        pair_bias = bias_ref[(j == qi).astype(jnp.int32)]
        tops = []
        for c in range(2):
            s = _dot_nt(kb, qm_ref[c, rows, :]) + pair_bias
            s_ref[slot, c] = s
            tops.append(jnp.max(s, axis=0, keepdims=True) + block_offset(qi, j))
        return tops

    def exp_pv(slot, step, qi, j, tops, maxima):
        vb = jnp.concatenate([vt_ref[0, j], ones_rows], axis=0)
        new_maxima = []
        for c in range(2):
            m_old = jnp.where(j == 0, -jnp.inf, maxima[c])
            m_new = jnp.maximum(m_old, tops[c])
            alpha = jnp.exp2(m_old - m_new)
            p = jnp.exp2(s_ref[slot, c] - (m_new - block_offset(qi, j)))
            acc = alpha * acc_ref[c] + _dot(vb, p.astype(_BF16))
            acc_ref[c] = acc
            fin_ref[step, c] = acc
            new_maxima.append(m_new)
        return new_maxima

    def finish(step, qi):
        w = HEAD_WIDTH
        o = (fin_ref[step, 0, 0:w] / fin_ref[step, 0, w:w + 1]
             - lam * (fin_ref[step, 1, 0:w] / fin_ref[step, 1, w:w + 1]))
        o = o * lax.rsqrt(jnp.mean(o * o, axis=0, keepdims=True) + NORM_EPS)
        rows = pl.ds(pl.multiple_of(qi * tq, tq), tq)
        o_ref[0, rows, :] = (o.T * sg_ref[...] * (1.0 - lambda_init)).astype(o_ref.dtype)

    def advance(qi, j):
        last = j == qi
        return jnp.where(last, qi + 1, qi), jnp.where(last, 0, j + 1)

    def trip(carry, final):
        pairs = [carry[0:2]]
        for _ in range(PAIRS_PER_TRIP):
            pairs.append(advance(*pairs[-1]))
        maxima, tops = carry[2:4], carry[4:6]
        for step in range(PAIRS_PER_TRIP):
            last_of_kernel = final and step == PAIRS_PER_TRIP - 1
            if not last_of_kernel:
                next_tops = scores((step + 1) % 2, *pairs[step + 1])
            maxima = exp_pv(step % 2, step, *pairs[step], tops, maxima)
            tops = None if last_of_kernel else next_tops
        for step in range(PAIRS_PER_TRIP):
            qi, j = pairs[step]
            if final and step == PAIRS_PER_TRIP - 1:
                finish(step, qi)
            else:
                pl.when(j == qi)(functools.partial(finish, step, qi))
        return None if final else (*pairs[PAIRS_PER_TRIP], *maxima, *tops)

    neg_inf = jnp.full((1, tq), -jnp.inf, _F32)
    n_pairs = n_q * (n_q + 1) // 2
    assert n_pairs % PAIRS_PER_TRIP == 0 and PAIRS_PER_TRIP % 2 == 0
    first = (jnp.int32(0), jnp.int32(0))
    carry = lax.fori_loop(0, n_pairs // PAIRS_PER_TRIP - 1, lambda _, c: trip(c, final=False),
                          (*first, neg_inf, neg_inf, *scores(0, *first)))
    trip(carry, final=True)


def _diff_attention(q, k, vt, slopes, lam_params, subln_g, lambda_init):
    b, s, d = q.shape
    n_kv = vt.shape[1]
    kern = functools.partial(_attn_kernel, lambda_init=lambda_init, tq=Q_TILE)
    head_rows = pl.BlockSpec((1, s, HEAD_WIDTH), lambda bi, hi: (bi, 0, hi))
    return pl.pallas_call(
        kern,
        grid=(b, N_HEADS),
        in_specs=[
            pl.BlockSpec(memory_space=pltpu.SMEM),
            head_rows,
            head_rows,
            pl.BlockSpec((1, n_kv, HEAD_WIDTH, KV_TILE), lambda bi, hi: (bi, 0, hi, 0)),
            pl.BlockSpec(lam_params.shape, lambda bi, hi: (0, 0)),
            pl.BlockSpec(subln_g.shape, lambda bi, hi: (0, 0)),
        ],
        out_specs=head_rows,
        out_shape=jax.ShapeDtypeStruct((b, s, d), _BF16),
        scratch_shapes=[pltpu.VMEM((2, s, HEAD_WIDTH), _BF16),
                        pltpu.VMEM((2, KV_TILE, Q_TILE), _F32),
                        pltpu.VMEM((2, 2, KV_TILE, Q_TILE), _F32),
                        pltpu.VMEM((2, HEAD_WIDTH + SUM_ROWS, Q_TILE), _F32),
                        pltpu.VMEM((PAIRS_PER_TRIP, 2, HEAD_WIDTH + SUM_ROWS, Q_TILE), _F32)],
        compiler_params=pltpu.CompilerParams(
            dimension_semantics=("parallel", "parallel"),
            vmem_limit_bytes=VMEM_LIMIT_BYTES),
        name="diff_attention",
    )(slopes, q, k, vt, lam_params, subln_g)


def _oproj_kernel(h_ref, o_ref, g_ref, wo_ref, out_ref):
    h = h_ref[...]
    out_ref[...] = h + _rms(_dot(o_ref[...], wo_ref[...]), g_ref[...])


def _o_proj(h, o, g, w_o):
    t, d = h.shape
    row = pl.BlockSpec((ROW_TILE, d), lambda i: (i, 0))
    return pl.pallas_call(
        _oproj_kernel,
        grid=(t // ROW_TILE,),
        in_specs=[row, row, _resident((1, d)), _resident(w_o.shape)],
        out_specs=row,
        out_shape=jax.ShapeDtypeStruct(h.shape, _F32),
        compiler_params=_row_params(("parallel",)),
        name="o_proj",
    )(h, o, g, w_o)


def kernel(x, ffn_w_gu, ffn_w_down, norm_g, conv_w_in, conv_k, conv_w_out, kv_norm_g, w_kv,
           attn_w_q, attn_lambda, attn_subln_g, attn_w_o):
    bsz, seq, d = x.shape
    assert d == D_MODEL and seq % ROW_TILE == 0 and seq % KV_TILE == 0 and Q_TILE == KV_TILE
    depth = ffn_w_gu.shape[0]
    assert depth == 2 * N_A_LAYERS

    bf = lambda w: w.astype(_BF16)
    g = lambda l, i: norm_g[l, i].reshape(1, d)
    h = x.reshape(bsz * seq, d)
    w_gu, w_down = bf(ffn_w_gu), bf(ffn_w_down)

    h = _ffn(h, g(0, 0), g(0, 1), w_gu, w_down, 0, 0)
    h = _conv_mixer(h, g(0, 2), g(0, 3), bf(conv_w_in[0]), conv_k[0], bf(conv_w_out[0]), seq)
    h = _ffn(h, g(0, 4), g(0, 5), w_gu, w_down, 0, 1)

    k, vt = _kv_proj(h, kv_norm_g.reshape(1, d), bf(w_kv[:, :d]), bf(w_kv[:, d:].T), seq)

    h = _ffn(h, g(1, 0), g(1, 1), w_gu, w_down, 1, 0)
    q = _q_proj(h, g(1, 2), bf(attn_w_q[0]))
    lambda_init = 0.8 - 0.6 * math.exp(-0.3 * 1)
    slopes = 2.0 ** (-8.0 * jnp.arange(1, N_HEADS + 1, dtype=_F32) / N_HEADS)
    o = _diff_attention(q.reshape(bsz, seq, d), k.reshape(bsz, seq, d), vt, slopes,
                        attn_lambda[0], attn_subln_g[0].reshape(1, HEAD_WIDTH), lambda_init)
    h = _o_proj(h, o.reshape(bsz * seq, d), g(1, 3), bf(attn_w_o[0]))
    h = _ffn(h, g(1, 4), g(1, 5), w_gu, w_down, 1, 1)
    return h.reshape(bsz, seq, d)
```

```python
import functools
import math

import jax
import jax.numpy as jnp
from jax import lax
from jax.experimental import pallas as pl
from jax.experimental.pallas import tpu as pltpu

D_MODEL = 1024
D_FF = 2816
HEAD_DIM = 64
HEAD_WIDTH = 2 * HEAD_DIM
N_HEADS = D_MODEL // HEAD_WIDTH
CONV_WIDTH = 3
NORM_EPS = 1e-6
N_A_LAYERS = 1

ROW_TILE = 1024
SUB_ROWS = 512
FF_CHUNK = 256
CONV_CHUNK = 256
CONV_HALO = 8
Q_TILE = 512
KV_TILE = 512
SUM_ROWS = 16
LOG2_E = math.log2(math.e)
VMEM_LIMIT_BYTES = 56 * 1024 * 1024

_F32 = jnp.float32
_BF16 = jnp.bfloat16


def _rms(x, g):
    return x * lax.rsqrt(jnp.mean(x * x, axis=-1, keepdims=True) + NORM_EPS) * g


def _dot(a, b):
    return jnp.dot(a, b, preferred_element_type=_F32)


def _dot_nt(a, b):
    return lax.dot_general(a, b, (((1,), (1,)), ((), ())), preferred_element_type=_F32)


def _resident(shape):
    return pl.BlockSpec(shape, lambda *_: (0,) * len(shape), pipeline_mode=pl.Buffered(1))


def _row_params(semantics):
    return pltpu.CompilerParams(dimension_semantics=semantics, vmem_limit_bytes=VMEM_LIMIT_BYTES)


def _ffn_kernel(h_ref, gpre_ref, gpost_ref, wgu_ref, wd_ref, o_ref):
    for r in range(h_ref.shape[0] // SUB_ROWS):
        rows = slice(r * SUB_ROWS, (r + 1) * SUB_ROWS)
        h = h_ref[rows, :]
        xn = _rms(h, gpre_ref[...]).astype(_BF16)
        acc = jnp.zeros(h.shape, _F32)
        for c in range(D_FF // FF_CHUNK):
            lo = c * FF_CHUNK
            gate = _dot(xn, wgu_ref[:, lo:lo + FF_CHUNK])
            up = _dot(xn, wgu_ref[:, D_FF + lo:D_FF + lo + FF_CHUNK])
            act = (jax.nn.silu(gate) * up).astype(_BF16)
            acc = acc + _dot(act, wd_ref[lo:lo + FF_CHUNK, :])
        o_ref[rows, :] = h + 0.5 * _rms(acc, gpost_ref[...])


def _ffn(h, g_pre, g_post, w_gu, w_down, layer, half):
    t, d = h.shape
    row = pl.BlockSpec((ROW_TILE, d), lambda i: (i, 0))

    def picked(w):
        return pl.BlockSpec((None, None) + w.shape[2:], lambda i: (layer, half, 0, 0),
                            pipeline_mode=pl.Buffered(1))

    return pl.pallas_call(
        _ffn_kernel,
        grid=(t // ROW_TILE,),
        in_specs=[row, _resident((1, d)), _resident((1, d)), picked(w_gu), picked(w_down)],
        out_specs=row,
        out_shape=jax.ShapeDtypeStruct(h.shape, _F32),
        compiler_params=_row_params(("parallel",)),
        name="ffn_half",
    )(h, g_pre, g_post, w_gu, w_down)


def _conv_kernel(h_ref, gpre_ref, gpost_ref, win_ref, ck_ref, wout_ref, o_ref, u_ref, *, tiles_per_seq):
    tm, d = h_ref.shape
    first_of_seq = pl.program_id(0) % tiles_per_seq == 0

    @pl.when(first_of_seq)
    def _():
        u_ref[0:CONV_HALO, :] = jnp.zeros((CONV_HALO, d), _F32)

    @pl.when(jnp.logical_not(first_of_seq))
    def _():
        u_ref[0:CONV_HALO, :] = u_ref[tm:tm + CONV_HALO, :]

    h = h_ref[...]
    xn = _rms(h, gpre_ref[...]).astype(_BF16)
    acc = jnp.zeros(h.shape, _F32)
    for c in range(d // CONV_CHUNK):
        lo = c * CONV_CHUNK
        cols = slice(lo, lo + CONV_CHUNK)
        b_gate = _dot(xn, win_ref[:, lo:lo + CONV_CHUNK])
        c_gate = _dot(xn, win_ref[:, d + lo:d + lo + CONV_CHUNK])
        z = _dot(xn, win_ref[:, 2 * d + lo:2 * d + lo + CONV_CHUNK])
        u = c_gate * z
        u_ref[CONV_HALO:CONV_HALO + tm, cols] = u
        conv = (ck_ref[2:3, cols] * u
                + ck_ref[1:2, cols] * u_ref[CONV_HALO - 1:CONV_HALO - 1 + tm, cols]
                + ck_ref[0:1, cols] * u_ref[CONV_HALO - 2:CONV_HALO - 2 + tm, cols])
        gated = (b_gate * conv).astype(_BF16)
        acc = acc + _dot(gated, wout_ref[lo:lo + CONV_CHUNK, :])
    o_ref[...] = h + _rms(acc, gpost_ref[...])


def _conv_mixer(h, g_pre, g_post, w_in, conv_k, w_out, seq_len):
    t, d = h.shape
    row = pl.BlockSpec((ROW_TILE, d), lambda i: (i, 0))
    kern = functools.partial(_conv_kernel, tiles_per_seq=seq_len // ROW_TILE)
    return pl.pallas_call(
        kern,
        grid=(t // ROW_TILE,),
        in_specs=[row, _resident((1, d)), _resident((1, d)), _resident(w_in.shape),
                  _resident(conv_k.shape), _resident(w_out.shape)],
        out_specs=row,
        out_shape=jax.ShapeDtypeStruct(h.shape, _F32),
        scratch_shapes=[pltpu.VMEM((ROW_TILE + CONV_HALO, d), _F32)],
        compiler_params=_row_params(("arbitrary",)),
        name="conv_mixer",
    )(h, g_pre, g_post, w_in, conv_k, w_out)


def _kv_kernel(h_ref, g_ref, wk_ref, wvt_ref, k_ref, vt_ref):
    xn = _rms(h_ref[...], g_ref[...]).astype(_BF16)
    k_ref[...] = _dot(xn, wk_ref[...]).astype(_BF16)
    vt_ref[0, 0] = _dot_nt(wvt_ref[...], xn).astype(_BF16)


def _kv_proj(h, g, w_k, w_vt, seq_len):
    t, d = h.shape
    n_kv = seq_len // KV_TILE
    row = pl.BlockSpec((KV_TILE, d), lambda i: (i, 0))
    return pl.pallas_call(
        _kv_kernel,
        grid=(t // KV_TILE,),
        in_specs=[row, _resident((1, d)), _resident(w_k.shape), _resident(w_vt.shape)],
        out_specs=[row, pl.BlockSpec((1, 1, d, KV_TILE), lambda i: (i // n_kv, i % n_kv, 0, 0))],
        out_shape=[jax.ShapeDtypeStruct((t, d), _BF16),
                   jax.ShapeDtypeStruct((t // seq_len, n_kv, d, KV_TILE), _BF16)],
        compiler_params=_row_params(("parallel",)),
        name="kv_proj",
    )(h, g, w_k, w_vt)


def _q_kernel(h_ref, g_ref, wq_ref, q_ref):
    xn = _rms(h_ref[...], g_ref[...]).astype(_BF16)
    q_ref[...] = (_dot(xn, wq_ref[...]) * (HEAD_DIM ** -0.5 * LOG2_E)).astype(_BF16)


def _q_proj(h, g, w_q):
    t, d = h.shape
    row = pl.BlockSpec((ROW_TILE, d), lambda i: (i, 0))
    return pl.pallas_call(
        _q_kernel,
        grid=(t // ROW_TILE,),
        in_specs=[row, _resident((1, d)), _resident(w_q.shape)],
        out_specs=row,
        out_shape=jax.ShapeDtypeStruct((t, d), _BF16),
        compiler_params=_row_params(("parallel",)),
        name="q_proj",
    )(h, g, w_q)


def _add_key_bias(s, bias):
    return jnp.concatenate([s[:, lo:lo + 128] + bias for lo in range(0, s.shape[1], 128)], axis=1)


def _attn_kernel(slopes_ref, q_ref, k_ref, vt_ref, lam_ref, sg_ref, o_ref,
                 qm_ref, bias_ref, mask_ref, s_ref, acc_ref, *, lambda_init, tq):
    seq = q_ref.shape[1]
    tk = vt_ref.shape[3]
    assert tq == tk and tk % 2 == 0
    half = tk // 2
    n_q = seq // tq
    slope = slopes_ref[pl.program_id(1)] * LOG2_E

    bias_ref[...] = slope * lax.broadcasted_iota(jnp.int32, bias_ref.shape, 0).astype(_F32)
    kk = lax.broadcasted_iota(jnp.int32, (tk, tq), 0)
    qq = lax.broadcasted_iota(jnp.int32, (tk, tq), 1)
    mask_ref[...] = jnp.where(kk <= qq, slope * kk.astype(_F32), -jnp.inf)

    q = q_ref[0]
    lane = lax.broadcasted_iota(jnp.int32, q.shape, 1)
    zero = jnp.zeros_like(q)
    qm_ref[0] = jnp.where(lane < HEAD_DIM, q, zero)
    qm_ref[1] = jnp.where(lane >= HEAD_DIM, q, zero)

    ones_rows = jnp.ones((SUM_ROWS, tk), _BF16)

    lp = lam_ref[...]
    lam = (jnp.exp(jnp.sum(lp[0:1] * lp[1:2], axis=-1, keepdims=True))
           - jnp.exp(jnp.sum(lp[2:3] * lp[3:4], axis=-1, keepdims=True)) + lambda_init)

    def block_offset(qi, j):
        return slope * float(j * tk - qi * tq)

    def scores(slot, qi, j):
        kb = k_ref[0, j * tk:(j + 1) * tk, :]
        tops = []
        for c in range(2):
            qc = qm_ref[c, qi * tq:(qi + 1) * tq, :]
            if j == qi:
                upper = _dot_nt(kb[0:half], qc) + mask_ref[0:half, :]
                lower = _dot_nt(kb[half:tk], qc[half:tq]) + mask_ref[half:tk, half:tq]
                s_ref[slot, c, 0:half, :] = upper
                s_ref[slot, c, half:tk, half:tq] = lower
                left = jnp.max(upper[:, 0:half], axis=0, keepdims=True)
                right = jnp.maximum(jnp.max(upper[:, half:tq], axis=0, keepdims=True),
                                    jnp.max(lower, axis=0, keepdims=True))
                tops.append(jnp.concatenate([left, right], axis=1))
            else:
                s = _add_key_bias(_dot_nt(kb, qc), bias_ref[...])
                s_ref[slot, c] = s
                tops.append(jnp.max(s, axis=0, keepdims=True) + block_offset(qi, j))
        return tops

    def exp_pv(slot, qi, j, tops, maxima):
        vb = jnp.concatenate([vt_ref[0, j], ones_rows], axis=0)
        new_maxima = []
        for c in range(2):
            if j == qi:
                m_new = tops[c]
                p_upper = jnp.exp2(s_ref[slot, c, 0:half, :] - m_new).astype(_BF16)
                p_lower = jnp.exp2(s_ref[slot, c, half:tk, half:tq] - m_new[:, half:tq]).astype(_BF16)
                from_upper = _dot(vb[:, 0:half], p_upper)
                from_lower = _dot(vb[:, half:tk], p_lower)
                acc = jnp.concatenate([from_upper[:, 0:half], from_upper[:, half:tq] + from_lower], axis=1)
            else:
                m_new = jnp.maximum(maxima[c], tops[c])
                alpha = jnp.exp2(maxima[c] - m_new)
                p = jnp.exp2(s_ref[slot, c] - (m_new - block_offset(qi, j))).astype(_BF16)
                acc = alpha * acc_ref[qi % 2, c] + _dot(vb, p)
            acc_ref[qi % 2, c] = acc
            new_maxima.append(m_new)
        return new_maxima

    def finish(qi):
        w = HEAD_WIDTH
        acc0, acc1 = acc_ref[qi % 2, 0], acc_ref[qi % 2, 1]
        o = acc0[0:w] / acc0[w:w + 1] - lam * (acc1[0:w] / acc1[w:w + 1])
        o = o * lax.rsqrt(jnp.mean(o * o, axis=0, keepdims=True) + NORM_EPS)
        o_ref[0, qi * tq:(qi + 1) * tq, :] = (o.T * sg_ref[...] * (1.0 - lambda_init)).astype(o_ref.dtype)

    pairs = [(qi, j) for qi in range(n_q) for j in [qi] + list(range(qi))]
    tops = scores(0, *pairs[0])
    maxima = None
    for idx, (qi, j) in enumerate(pairs):
        if idx + 1 < len(pairs):
            next_tops = scores((idx + 1) % 2, *pairs[idx + 1])
        maxima = exp_pv(idx % 2, qi, j, tops, maxima)
        if j == (qi - 1 if qi else 0):
            finish(qi)
        tops = next_tops


def _diff_attention(q, k, vt, slopes, lam_params, subln_g, lambda_init):
    b, s, d = q.shape
    n_kv = vt.shape[1]
    kern = functools.partial(_attn_kernel, lambda_init=lambda_init, tq=Q_TILE)
    head_rows = pl.BlockSpec((1, s, HEAD_WIDTH), lambda bi, hi: (bi, 0, hi))
    return pl.pallas_call(
        kern,
        grid=(b, N_HEADS),
        in_specs=[
            pl.BlockSpec(memory_space=pltpu.SMEM),
            head_rows,
            head_rows,
            pl.BlockSpec((1, n_kv, HEAD_WIDTH, KV_TILE), lambda bi, hi: (bi, 0, hi, 0)),
            pl.BlockSpec(lam_params.shape, lambda bi, hi: (0, 0)),
            pl.BlockSpec(subln_g.shape, lambda bi, hi: (0, 0)),
        ],
        out_specs=head_rows,
        out_shape=jax.ShapeDtypeStruct((b, s, d), _BF16),
        scratch_shapes=[pltpu.VMEM((2, s, HEAD_WIDTH), _BF16),
                        pltpu.VMEM((KV_TILE, 128), _F32),
                        pltpu.VMEM((KV_TILE, Q_TILE), _F32),
                        pltpu.VMEM((2, 2, KV_TILE, Q_TILE), _F32),
                        pltpu.VMEM((2, 2, HEAD_WIDTH + SUM_ROWS, Q_TILE), _F32)],
        compiler_params=pltpu.CompilerParams(
            dimension_semantics=("parallel", "parallel"),
            vmem_limit_bytes=VMEM_LIMIT_BYTES),
        name="diff_attention",
    )(slopes, q, k, vt, lam_params, subln_g)


def _oproj_kernel(h_ref, o_ref, g_ref, wo_ref, out_ref):
    h = h_ref[...]
    out_ref[...] = h + _rms(_dot(o_ref[...], wo_ref[...]), g_ref[...])


def _o_proj(h, o, g, w_o):
    t, d = h.shape
    row = pl.BlockSpec((ROW_TILE, d), lambda i: (i, 0))
    return pl.pallas_call(
        _oproj_kernel,
        grid=(t // ROW_TILE,),
        in_specs=[row, row, _resident((1, d)), _resident(w_o.shape)],
        out_specs=row,
        out_shape=jax.ShapeDtypeStruct(h.shape, _F32),
        compiler_params=_row_params(("parallel",)),
        name="o_proj",
    )(h, o, g, w_o)


def kernel(x, ffn_w_gu, ffn_w_down, norm_g, conv_w_in, conv_k, conv_w_out, kv_norm_g, w_kv,
           attn_w_q, attn_lambda, attn_subln_g, attn_w_o):
    bsz, seq, d = x.shape
    assert d == D_MODEL and seq % ROW_TILE == 0 and seq % KV_TILE == 0 and Q_TILE == KV_TILE
    depth = ffn_w_gu.shape[0]
    assert depth == 2 * N_A_LAYERS

    bf = lambda w: w.astype(_BF16)
    g = lambda l, i: norm_g[l, i].reshape(1, d)
    h = x.reshape(bsz * seq, d)
    w_gu, w_down = bf(ffn_w_gu), bf(ffn_w_down)

    h = _ffn(h, g(0, 0), g(0, 1), w_gu, w_down, 0, 0)
    h = _conv_mixer(h, g(0, 2), g(0, 3), bf(conv_w_in[0]), conv_k[0], bf(conv_w_out[0]), seq)
    h = _ffn(h, g(0, 4), g(0, 5), w_gu, w_down, 0, 1)

    k, vt = _kv_proj(h, kv_norm_g.reshape(1, d), bf(w_kv[:, :d]), bf(w_kv[:, d:].T), seq)

    h = _ffn(h, g(1, 0), g(1, 1), w_gu, w_down, 1, 0)
    q = _q_proj(h, g(1, 2), bf(attn_w_q[0]))
    lambda_init = 0.8 - 0.6 * math.exp(-0.3 * 1)
    slopes = 2.0 ** (-8.0 * jnp.arange(1, N_HEADS + 1, dtype=_F32) / N_HEADS)
    o = _diff_attention(q.reshape(bsz, seq, d), k.reshape(bsz, seq, d), vt, slopes,
                        attn_lambda[0], attn_subln_g[0].reshape(1, HEAD_WIDTH), lambda_init)
    h = _o_proj(h, o.reshape(bsz * seq, d), g(1, 3), bf(attn_w_o[0]))
    h = _ffn(h, g(1, 4), g(1, 5), w_gu, w_down, 1, 1)
    return h.reshape(bsz, seq, d)
```

```python
import functools
import math

import jax
import jax.numpy as jnp
from jax import lax
from jax.experimental import pallas as pl
from jax.experimental.pallas import tpu as pltpu

D_MODEL = 1024
D_FF = 2816
HEAD_DIM = 64
HEAD_WIDTH = 2 * HEAD_DIM
N_HEADS = D_MODEL // HEAD_WIDTH
CONV_WIDTH = 3
NORM_EPS = 1e-6
N_A_LAYERS = 1

ROW_TILE = 1024
SUB_ROWS = 512
FF_CHUNK = 256
CONV_CHUNK = 256
CONV_HALO = 8
Q_TILE = 512
KV_TILE = 512
SUM_ROWS = 16
LOG2_E = math.log2(math.e)
VMEM_LIMIT_BYTES = 56 * 1024 * 1024

_F32 = jnp.float32
_BF16 = jnp.bfloat16


def _rms(x, g):
    return x * lax.rsqrt(jnp.mean(x * x, axis=-1, keepdims=True) + NORM_EPS) * g


def _dot(a, b):
    return jnp.dot(a, b, preferred_element_type=_F32)


def _dot_nt(a, b):
    return lax.dot_general(a, b, (((1,), (1,)), ((), ())), preferred_element_type=_F32)


def _resident(shape):
    return pl.BlockSpec(shape, lambda *_: (0,) * len(shape), pipeline_mode=pl.Buffered(1))


def _row_params(semantics):
    return pltpu.CompilerParams(dimension_semantics=semantics, vmem_limit_bytes=VMEM_LIMIT_BYTES)


def _ffn_kernel(h_ref, gpre_ref, gpost_ref, wgu_ref, wd_ref, o_ref):
    for r in range(h_ref.shape[0] // SUB_ROWS):
        rows = slice(r * SUB_ROWS, (r + 1) * SUB_ROWS)
        h = h_ref[rows, :]
        xn = _rms(h, gpre_ref[...]).astype(_BF16)
        acc = jnp.zeros(h.shape, _F32)
        for c in range(D_FF // FF_CHUNK):
            lo = c * FF_CHUNK
            gate = _dot(xn, wgu_ref[:, lo:lo + FF_CHUNK].astype(_BF16))
            up = _dot(xn, wgu_ref[:, D_FF + lo:D_FF + lo + FF_CHUNK].astype(_BF16))
            act = (jax.nn.silu(gate) * up).astype(_BF16)
            acc = acc + _dot(act, wd_ref[lo:lo + FF_CHUNK, :].astype(_BF16))
        o_ref[rows, :] = h + 0.5 * _rms(acc, gpost_ref[...])


def _ffn(h, g_pre, g_post, w_gu, w_down, layer, half):
    t, d = h.shape
    row = pl.BlockSpec((ROW_TILE, d), lambda i: (i, 0))

    def picked(w):
        return pl.BlockSpec((None, None) + w.shape[2:], lambda i: (layer, half, 0, 0),
                            pipeline_mode=pl.Buffered(1))

    return pl.pallas_call(
        _ffn_kernel,
        grid=(t // ROW_TILE,),
        in_specs=[row, _resident((1, d)), _resident((1, d)), picked(w_gu), picked(w_down)],
        out_specs=row,
        out_shape=jax.ShapeDtypeStruct(h.shape, _F32),
        compiler_params=_row_params(("parallel",)),
        name="ffn_half",
    )(h, g_pre, g_post, w_gu, w_down)


def _conv_kernel(h_ref, gpre_ref, gpost_ref, win_ref, ck_ref, wout_ref, o_ref, u_ref, *, tiles_per_seq):
    tm, d = h_ref.shape
    first_of_seq = pl.program_id(0) % tiles_per_seq == 0

    @pl.when(first_of_seq)
    def _():
        u_ref[0:CONV_HALO, :] = jnp.zeros((CONV_HALO, d), _F32)

    @pl.when(jnp.logical_not(first_of_seq))
    def _():
        u_ref[0:CONV_HALO, :] = u_ref[tm:tm + CONV_HALO, :]

    h = h_ref[...]
    xn = _rms(h, gpre_ref[...]).astype(_BF16)
    acc = jnp.zeros(h.shape, _F32)
    for c in range(d // CONV_CHUNK):
        lo = c * CONV_CHUNK
        cols = slice(lo, lo + CONV_CHUNK)
        b_gate = _dot(xn, win_ref[:, lo:lo + CONV_CHUNK])
        c_gate = _dot(xn, win_ref[:, d + lo:d + lo + CONV_CHUNK])
        z = _dot(xn, win_ref[:, 2 * d + lo:2 * d + lo + CONV_CHUNK])
        u = c_gate * z
        u_ref[CONV_HALO:CONV_HALO + tm, cols] = u
        conv = (ck_ref[2:3, cols] * u
                + ck_ref[1:2, cols] * u_ref[CONV_HALO - 1:CONV_HALO - 1 + tm, cols]
                + ck_ref[0:1, cols] * u_ref[CONV_HALO - 2:CONV_HALO - 2 + tm, cols])
        gated = (b_gate * conv).astype(_BF16)
        acc = acc + _dot(gated, wout_ref[lo:lo + CONV_CHUNK, :])
    o_ref[...] = h + _rms(acc, gpost_ref[...])


def _conv_mixer(h, g_pre, g_post, w_in, conv_k, w_out, seq_len):
    t, d = h.shape
    row = pl.BlockSpec((ROW_TILE, d), lambda i: (i, 0))
    kern = functools.partial(_conv_kernel, tiles_per_seq=seq_len // ROW_TILE)
    return pl.pallas_call(
        kern,
        grid=(t // ROW_TILE,),
        in_specs=[row, _resident((1, d)), _resident((1, d)), _resident(w_in.shape),
                  _resident(conv_k.shape), _resident(w_out.shape)],
        out_specs=row,
        out_shape=jax.ShapeDtypeStruct(h.shape, _F32),
        scratch_shapes=[pltpu.VMEM((ROW_TILE + CONV_HALO, d), _F32)],
        compiler_params=_row_params(("arbitrary",)),
        name="conv_mixer",
    )(h, g_pre, g_post, w_in, conv_k, w_out)


def _kv_kernel(h_ref, g_ref, wk_ref, wvt_ref, k_ref, vt_ref):
    xn = _rms(h_ref[...], g_ref[...]).astype(_BF16)
    k_ref[...] = _dot(xn, wk_ref[...]).astype(_BF16)
    vt_ref[0, 0] = _dot_nt(wvt_ref[...], xn).astype(_BF16)


def _kv_proj(h, g, w_k, w_vt, seq_len):
    t, d = h.shape
    n_kv = seq_len // KV_TILE
    row = pl.BlockSpec((KV_TILE, d), lambda i: (i, 0))
    return pl.pallas_call(
        _kv_kernel,
        grid=(t // KV_TILE,),
        in_specs=[row, _resident((1, d)), _resident(w_k.shape), _resident(w_vt.shape)],
        out_specs=[row, pl.BlockSpec((1, 1, d, KV_TILE), lambda i: (i // n_kv, i % n_kv, 0, 0))],
        out_shape=[jax.ShapeDtypeStruct((t, d), _BF16),
                   jax.ShapeDtypeStruct((t // seq_len, n_kv, d, KV_TILE), _BF16)],
        compiler_params=_row_params(("parallel",)),
        name="kv_proj",
    )(h, g, w_k, w_vt)


def _q_kernel(h_ref, g_ref, wq_ref, q_ref):
    xn = _rms(h_ref[...], g_ref[...]).astype(_BF16)
    q_ref[...] = (_dot(xn, wq_ref[...]) * (HEAD_DIM ** -0.5 * LOG2_E)).astype(_BF16)


def _q_proj(h, g, w_q):
    t, d = h.shape
    row = pl.BlockSpec((ROW_TILE, d), lambda i: (i, 0))
    return pl.pallas_call(
        _q_kernel,
        grid=(t // ROW_TILE,),
        in_specs=[row, _resident((1, d)), _resident(w_q.shape)],
        out_specs=row,
        out_shape=jax.ShapeDtypeStruct((t, d), _BF16),
        compiler_params=_row_params(("parallel",)),
        name="q_proj",
    )(h, g, w_q)


def _add_key_bias(s, bias):
    return jnp.concatenate([s[:, lo:lo + 128] + bias for lo in range(0, s.shape[1], 128)], axis=1)


def _attn_kernel(slopes_ref, q_ref, k_ref, vt_ref, lam_ref, sg_ref, o_ref,
                 qm_ref, bias_ref, mask_ref, s_ref, acc_ref, *, lambda_init, tq):
    seq = q_ref.shape[1]
    tk = vt_ref.shape[3]
    assert tq == tk and tk % 2 == 0
    half = tk // 2
    n_q = seq // tq
    slope = slopes_ref[pl.program_id(1)] * LOG2_E

    bias_ref[...] = slope * lax.broadcasted_iota(jnp.int32, bias_ref.shape, 0).astype(_F32)
    kk = lax.broadcasted_iota(jnp.int32, (tk, tq), 0)
    qq = lax.broadcasted_iota(jnp.int32, (tk, tq), 1)
    mask_ref[...] = jnp.where(kk <= qq, slope * kk.astype(_F32), -jnp.inf)

    q = q_ref[0]
    lane = lax.broadcasted_iota(jnp.int32, q.shape, 1)
    zero = jnp.zeros_like(q)
    qm_ref[0] = jnp.where(lane < HEAD_DIM, q, zero)
    qm_ref[1] = jnp.where(lane >= HEAD_DIM, q, zero)

    ones_rows = jnp.ones((SUM_ROWS, tk), _BF16)

    lp = lam_ref[...]
    lam = (jnp.exp(jnp.sum(lp[0:1] * lp[1:2], axis=-1, keepdims=True))
           - jnp.exp(jnp.sum(lp[2:3] * lp[3:4], axis=-1, keepdims=True)) + lambda_init)

    def block_offset(qi, j):
        return slope * float(j * tk - qi * tq)

    def scores(slot, qi, j):
        kb = k_ref[0, j * tk:(j + 1) * tk, :]
        tops = []
        for c in range(2):
            qc = qm_ref[c, qi * tq:(qi + 1) * tq, :]
            if j == qi:
                upper = _dot_nt(kb[0:half], qc) + mask_ref[0:half, :]
                lower = _dot_nt(kb[half:tk], qc[half:tq]) + mask_ref[half:tk, half:tq]
                s_ref[slot, c, 0:half, :] = upper
                s_ref[slot, c, half:tk, half:tq] = lower
                left = jnp.max(upper[:, 0:half], axis=0, keepdims=True)
                right = jnp.maximum(jnp.max(upper[:, half:tq], axis=0, keepdims=True),
                                    jnp.max(lower, axis=0, keepdims=True))
                tops.append(jnp.concatenate([left, right], axis=1))
            else:
                s = _add_key_bias(_dot_nt(kb, qc), bias_ref[...])
                s_ref[slot, c] = s
                tops.append(jnp.max(s, axis=0, keepdims=True) + block_offset(qi, j))
        return tops

    def exp_pv(slot, qi, j, tops, maxima):
        vb = jnp.concatenate([vt_ref[0, j], ones_rows], axis=0)
        new_maxima = []
        for c in range(2):
            if j == qi:
                m_new = tops[c]
                p_upper = jnp.exp2(s_ref[slot, c, 0:half, :] - m_new).astype(_BF16)
                p_lower = jnp.exp2(s_ref[slot, c, half:tk, half:tq] - m_new[:, half:tq]).astype(_BF16)
                from_upper = _dot(vb[:, 0:half], p_upper)
                from_lower = _dot(vb[:, half:tk], p_lower)
                acc = jnp.concatenate([from_upper[:, 0:half], from_upper[:, half:tq] + from_lower], axis=1)
            else:
                m_new = jnp.maximum(maxima[c], tops[c])
                alpha = jnp.exp2(maxima[c] - m_new)
                p = jnp.exp2(s_ref[slot, c] - (m_new - block_offset(qi, j))).astype(_BF16)
                acc = alpha * acc_ref[qi % 2, c] + _dot(vb, p)
            acc_ref[qi % 2, c] = acc
            new_maxima.append(m_new)
        return new_maxima

    def finish(qi):
        w = HEAD_WIDTH
        acc0, acc1 = acc_ref[qi % 2, 0], acc_ref[qi % 2, 1]
        o = acc0[0:w] / acc0[w:w + 1] - lam * (acc1[0:w] / acc1[w:w + 1])
        o = o * lax.rsqrt(jnp.mean(o * o, axis=0, keepdims=True) + NORM_EPS)
        o_ref[0, qi * tq:(qi + 1) * tq, :] = (o.T * sg_ref[...] * (1.0 - lambda_init)).astype(o_ref.dtype)

    pairs = [(qi, j) for qi in range(n_q) for j in [qi] + list(range(qi))]
    tops = scores(0, *pairs[0])
    maxima = None
    for idx, (qi, j) in enumerate(pairs):
        if idx + 1 < len(pairs):
            next_tops = scores((idx + 1) % 2, *pairs[idx + 1])
        maxima = exp_pv(idx % 2, qi, j, tops, maxima)
        if j == (qi - 1 if qi else 0):
            finish(qi)
        tops = next_tops


def _diff_attention(q, k, vt, slopes, lam_params, subln_g, lambda_init):
    b, s, d = q.shape
    n_kv = vt.shape[1]
    kern = functools.partial(_attn_kernel, lambda_init=lambda_init, tq=Q_TILE)
    head_rows = pl.BlockSpec((1, s, HEAD_WIDTH), lambda bi, hi: (bi, 0, hi))
    return pl.pallas_call(
        kern,
        grid=(b, N_HEADS),
        in_specs=[
            pl.BlockSpec(memory_space=pltpu.SMEM),
            head_rows,
            head_rows,
            pl.BlockSpec((1, n_kv, HEAD_WIDTH, KV_TILE), lambda bi, hi: (bi, 0, hi, 0)),
            pl.BlockSpec(lam_params.shape, lambda bi, hi: (0, 0)),
            pl.BlockSpec(subln_g.shape, lambda bi, hi: (0, 0)),
        ],
        out_specs=head_rows,
        out_shape=jax.ShapeDtypeStruct((b, s, d), _BF16),
        scratch_shapes=[pltpu.VMEM((2, s, HEAD_WIDTH), _BF16),
                        pltpu.VMEM((KV_TILE, 128), _F32),
                        pltpu.VMEM((KV_TILE, Q_TILE), _F32),
                        pltpu.VMEM((2, 2, KV_TILE, Q_TILE), _F32),
                        pltpu.VMEM((2, 2, HEAD_WIDTH + SUM_ROWS, Q_TILE), _F32)],
        compiler_params=pltpu.CompilerParams(
            dimension_semantics=("parallel", "parallel"),
            vmem_limit_bytes=VMEM_LIMIT_BYTES),
        name="diff_attention",
    )(slopes, q, k, vt, lam_params, subln_g)


def _oproj_kernel(h_ref, o_ref, g_ref, wo_ref, out_ref):
    h = h_ref[...]
    out_ref[...] = h + _rms(_dot(o_ref[...], wo_ref[...]), g_ref[...])


def _o_proj(h, o, g, w_o):
    t, d = h.shape
    row = pl.BlockSpec((ROW_TILE, d), lambda i: (i, 0))
    return pl.pallas_call(
        _oproj_kernel,
        grid=(t // ROW_TILE,),
        in_specs=[row, row, _resident((1, d)), _resident(w_o.shape)],
        out_specs=row,
        out_shape=jax.ShapeDtypeStruct(h.shape, _F32),
        compiler_params=_row_params(("parallel",)),
        name="o_proj",
    )(h, o, g, w_o)


def kernel(x, ffn_w_gu, ffn_w_down, norm_g, conv_w_in, conv_k, conv_w_out, kv_norm_g, w_kv,
           attn_w_q, attn_lambda, attn_subln_g, attn_w_o):
    bsz, seq, d = x.shape
    assert d == D_MODEL and seq % ROW_TILE == 0 and seq % KV_TILE == 0 and Q_TILE == KV_TILE
    depth = ffn_w_gu.shape[0]
    assert depth == 2 * N_A_LAYERS

    bf = lambda w: w.astype(_BF16)
    g = lambda l, i: norm_g[l, i].reshape(1, d)
    h = x.reshape(bsz * seq, d)
    w_gu, w_down = ffn_w_gu, ffn_w_down

    h = _ffn(h, g(0, 0), g(0, 1), w_gu, w_down, 0, 0)
    h = _conv_mixer(h, g(0, 2), g(0, 3), bf(conv_w_in[0]), conv_k[0], bf(conv_w_out[0]), seq)
    h = _ffn(h, g(0, 4), g(0, 5), w_gu, w_down, 0, 1)

    k, vt = _kv_proj(h, kv_norm_g.reshape(1, d), bf(w_kv[:, :d]), bf(w_kv[:, d:].T), seq)

    h = _ffn(h, g(1, 0), g(1, 1), w_gu, w_down, 1, 0)
    q = _q_proj(h, g(1, 2), bf(attn_w_q[0]))
    lambda_init = 0.8 - 0.6 * math.exp(-0.3 * 1)
    slopes = 2.0 ** (-8.0 * jnp.arange(1, N_HEADS + 1, dtype=_F32) / N_HEADS)
    o = _diff_attention(q.reshape(bsz, seq, d), k.reshape(bsz, seq, d), vt, slopes,
                        attn_lambda[0], attn_subln_g[0].reshape(1, HEAD_WIDTH), lambda_init)
    h = _o_proj(h, o.reshape(bsz * seq, d), g(1, 3), bf(attn_w_o[0]))
    h = _ffn(h, g(1, 4), g(1, 5), w_gu, w_down, 1, 1)
    return h.reshape(bsz, seq, d)
```

```python
import functools
import math

import jax
import jax.numpy as jnp
from jax import lax
from jax.experimental import pallas as pl
from jax.experimental.pallas import tpu as pltpu

D_MODEL = 1024
D_FF = 2816
HEAD_DIM = 64
HEAD_WIDTH = 2 * HEAD_DIM
N_HEADS = D_MODEL // HEAD_WIDTH
CONV_WIDTH = 3
NORM_EPS = 1e-6
N_A_LAYERS = 1

LANES = 128
SUBLANES = 8
ROW_TILE = 1024
SUB_ROWS = 512
NORM_GROUPS = 8
FF_CHUNK = 256
CONV_CHUNK = 256
CONV_HALO = 8
Q_TILE = 512
KV_TILE = 512
SUM_ROWS = 16
LOG2_E = math.log2(math.e)
VMEM_LIMIT_BYTES = 56 * 1024 * 1024

_F32 = jnp.float32
_BF16 = jnp.bfloat16


def _rms(x, g):
    return x * lax.rsqrt(jnp.mean(x * x, axis=-1, keepdims=True) + NORM_EPS) * g


def _dot(a, b):
    return jnp.dot(a, b, preferred_element_type=_F32)


def _dot_nt(a, b):
    return lax.dot_general(a, b, (((1,), (1,)), ((), ())), preferred_element_type=_F32)


def _resident(shape):
    return pl.BlockSpec(shape, lambda *_: (0,) * len(shape), pipeline_mode=pl.Buffered(1))


def _row_params(semantics):
    return pltpu.CompilerParams(dimension_semantics=semantics, vmem_limit_bytes=VMEM_LIMIT_BYTES)


def _max_tiles(x, rows):
    tile = None
    for i in range(0, x.shape[0], rows):
        for j in range(0, x.shape[1], LANES):
            part = x[i:i + rows, j:j + LANES]
            tile = part if tile is None else jnp.maximum(tile, part)
    return tile


def _after(x, anchor):
    rows = SUBLANES * 4 // x.dtype.itemsize
    tile = _max_tiles(anchor, rows).astype(x.dtype)
    other = jnp.tile(tile, (x.shape[0] // rows, x.shape[1] // LANES))
    return jnp.where(pl.program_id(0) >= 0, x, other)


def _ffn_kernel(h_ref, gpre_ref, gpost_ref, wgu_ref, wd_ref, o_ref, xn_ref, acc_ref):
    n_sub = h_ref.shape[0] // SUB_ROWS
    group = SUB_ROWS // NORM_GROUPS

    def rows(r, g):
        return slice(r * SUB_ROWS + g * group, r * SUB_ROWS + (g + 1) * group)

    def pre(r):
        sub = slice(r * SUB_ROWS, (r + 1) * SUB_ROWS)
        xn_ref[r] = _rms(h_ref[sub, :], gpre_ref[...]).astype(_BF16)

    def post(r, g, anchor=None):
        acc = acc_ref[r, g * group:(g + 1) * group, :]
        if anchor is not None:
            acc = _after(acc, anchor)
        o_ref[rows(r, g), :] = h_ref[rows(r, g), :] + 0.5 * _rms(acc, gpost_ref[...])

    pre(0)
    for r in range(n_sub):
        if r + 1 < n_sub:
            pre(r + 1)
        xn = xn_ref[r]
        acc = jnp.zeros((SUB_ROWS, h_ref.shape[1]), _F32)
        for c in range(D_FF // FF_CHUNK):
            lo = c * FF_CHUNK
            gate = _dot(xn, wgu_ref[:, lo:lo + FF_CHUNK].astype(_BF16))
            up = _dot(xn, wgu_ref[:, D_FF + lo:D_FF + lo + FF_CHUNK].astype(_BF16))
            act = (jax.nn.silu(gate) * up).astype(_BF16)
            g = c - 1
            if 0 <= g < NORM_GROUPS:
                if r + 1 < n_sub:
                    act = _after(act, xn_ref[r + 1, g * group:(g + 1) * group, :])
                if r > 0:
                    post(r - 1, g, anchor=acc[0:SUBLANES, :])
            acc = acc + _dot(act, wd_ref[lo:lo + FF_CHUNK, :].astype(_BF16))
        acc_ref[r] = acc
    for g in range(NORM_GROUPS):
        post(n_sub - 1, g)


def _ffn(h, g_pre, g_post, w_gu, w_down, layer, half):
    t, d = h.shape
    row = pl.BlockSpec((ROW_TILE, d), lambda i: (i, 0))

    def picked(w):
        return pl.BlockSpec((None, None) + w.shape[2:], lambda i: (layer, half, 0, 0),
                            pipeline_mode=pl.Buffered(1))

    return pl.pallas_call(
        _ffn_kernel,
        grid=(t // ROW_TILE,),
        in_specs=[row, _resident((1, d)), _resident((1, d)), picked(w_gu), picked(w_down)],
        out_specs=row,
        out_shape=jax.ShapeDtypeStruct(h.shape, _F32),
        scratch_shapes=[pltpu.VMEM((ROW_TILE // SUB_ROWS, SUB_ROWS, d), _BF16),
                        pltpu.VMEM((ROW_TILE // SUB_ROWS, SUB_ROWS, d), _F32)],
        compiler_params=_row_params(("parallel",)),
        name="ffn_half",
    )(h, g_pre, g_post, w_gu, w_down)


def _conv_kernel(h_ref, gpre_ref, gpost_ref, win_ref, ck_ref, wout_ref, o_ref, u_ref, *, tiles_per_seq):
    tm, d = h_ref.shape
    first_of_seq = pl.program_id(0) % tiles_per_seq == 0

    @pl.when(first_of_seq)
    def _():
        u_ref[0:CONV_HALO, :] = jnp.zeros((CONV_HALO, d), _F32)

    @pl.when(jnp.logical_not(first_of_seq))
    def _():
        u_ref[0:CONV_HALO, :] = u_ref[tm:tm + CONV_HALO, :]

    h = h_ref[...]
    xn = _rms(h, gpre_ref[...]).astype(_BF16)
    acc = jnp.zeros(h.shape, _F32)
    for c in range(d // CONV_CHUNK):
        lo = c * CONV_CHUNK
        cols = slice(lo, lo + CONV_CHUNK)
        b_gate = _dot(xn, win_ref[:, lo:lo + CONV_CHUNK])
        c_gate = _dot(xn, win_ref[:, d + lo:d + lo + CONV_CHUNK])
        z = _dot(xn, win_ref[:, 2 * d + lo:2 * d + lo + CONV_CHUNK])
        u = c_gate * z
        u_ref[CONV_HALO:CONV_HALO + tm, cols] = u
        conv = (ck_ref[2:3, cols] * u
                + ck_ref[1:2, cols] * u_ref[CONV_HALO - 1:CONV_HALO - 1 + tm, cols]
                + ck_ref[0:1, cols] * u_ref[CONV_HALO - 2:CONV_HALO - 2 + tm, cols])
        gated = (b_gate * conv).astype(_BF16)
        acc = acc + _dot(gated, wout_ref[lo:lo + CONV_CHUNK, :])
    o_ref[...] = h + _rms(acc, gpost_ref[...])


def _conv_mixer(h, g_pre, g_post, w_in, conv_k, w_out, seq_len):
    t, d = h.shape
    row = pl.BlockSpec((ROW_TILE, d), lambda i: (i, 0))
    kern = functools.partial(_conv_kernel, tiles_per_seq=seq_len // ROW_TILE)
    return pl.pallas_call(
        kern,
        grid=(t // ROW_TILE,),
        in_specs=[row, _resident((1, d)), _resident((1, d)), _resident(w_in.shape),
                  _resident(conv_k.shape), _resident(w_out.shape)],
        out_specs=row,
        out_shape=jax.ShapeDtypeStruct(h.shape, _F32),
        scratch_shapes=[pltpu.VMEM((ROW_TILE + CONV_HALO, d), _F32)],
        compiler_params=_row_params(("arbitrary",)),
        name="conv_mixer",
    )(h, g_pre, g_post, w_in, conv_k, w_out)


def _kv_kernel(h_ref, g_ref, wk_ref, wvt_ref, k_ref, vt_ref):
    xn = _rms(h_ref[...], g_ref[...]).astype(_BF16)
    k_ref[...] = _dot(xn, wk_ref[...]).astype(_BF16)
    vt_ref[0, 0] = _dot_nt(wvt_ref[...], xn).astype(_BF16)


def _kv_proj(h, g, w_k, w_vt, seq_len):
    t, d = h.shape
    n_kv = seq_len // KV_TILE
    row = pl.BlockSpec((KV_TILE, d), lambda i: (i, 0))
    return pl.pallas_call(
        _kv_kernel,
        grid=(t // KV_TILE,),
        in_specs=[row, _resident((1, d)), _resident(w_k.shape), _resident(w_vt.shape)],
        out_specs=[row, pl.BlockSpec((1, 1, d, KV_TILE), lambda i: (i // n_kv, i % n_kv, 0, 0))],
        out_shape=[jax.ShapeDtypeStruct((t, d), _BF16),
                   jax.ShapeDtypeStruct((t // seq_len, n_kv, d, KV_TILE), _BF16)],
        compiler_params=_row_params(("parallel",)),
        name="kv_proj",
    )(h, g, w_k, w_vt)


def _q_kernel(h_ref, g_ref, wq_ref, q_ref):
    xn = _rms(h_ref[...], g_ref[...]).astype(_BF16)
    q_ref[...] = (_dot(xn, wq_ref[...]) * (HEAD_DIM ** -0.5 * LOG2_E)).astype(_BF16)


def _q_proj(h, g, w_q):
    t, d = h.shape
    row = pl.BlockSpec((ROW_TILE, d), lambda i: (i, 0))
    return pl.pallas_call(
        _q_kernel,
        grid=(t // ROW_TILE,),
        in_specs=[row, _resident((1, d)), _resident(w_q.shape)],
        out_specs=row,
        out_shape=jax.ShapeDtypeStruct((t, d), _BF16),
        compiler_params=_row_params(("parallel",)),
        name="q_proj",
    )(h, g, w_q)


def _add_key_bias(s, bias):
    return jnp.concatenate([s[:, lo:lo + LANES] + bias for lo in range(0, s.shape[1], LANES)], axis=1)


def _attn_kernel(slopes_ref, q_ref, k_ref, vt_ref, lam_ref, sg_ref, o_ref,
                 qm_ref, bias_ref, mask_ref, s_ref, acc_ref, *, lambda_init, tq):
    seq = q_ref.shape[1]
    tk = vt_ref.shape[3]
    assert tq == tk and tk % 2 == 0
    half = tk // 2
    n_q = seq // tq
    slope = slopes_ref[pl.program_id(1)] * LOG2_E

    bias_ref[...] = slope * lax.broadcasted_iota(jnp.int32, bias_ref.shape, 0).astype(_F32)
    kk = lax.broadcasted_iota(jnp.int32, (tk, tq), 0)
    qq = lax.broadcasted_iota(jnp.int32, (tk, tq), 1)
    mask_ref[...] = jnp.where(kk <= qq, slope * kk.astype(_F32), -jnp.inf)

    q = q_ref[0]
    lane = lax.broadcasted_iota(jnp.int32, q.shape, 1)
    zero = jnp.zeros_like(q)
    qm_ref[0] = jnp.where(lane < HEAD_DIM, q, zero)
    qm_ref[1] = jnp.where(lane >= HEAD_DIM, q, zero)

    ones_rows = jnp.ones((SUM_ROWS, tk), _BF16)

    lp = lam_ref[...]
    lam = (jnp.exp(jnp.sum(lp[0:1] * lp[1:2], axis=-1, keepdims=True))
           - jnp.exp(jnp.sum(lp[2:3] * lp[3:4], axis=-1, keepdims=True)) + lambda_init)

    def block_offset(qi, j):
        return slope * float(j * tk - qi * tq)

    def scores(slot, qi, j):
        kb = k_ref[0, j * tk:(j + 1) * tk, :]
        tops = []
        for c in range(2):
            qc = qm_ref[c, qi * tq:(qi + 1) * tq, :]
            if j == qi:
                upper = _dot_nt(kb[0:half], qc) + mask_ref[0:half, :]
                lower = _dot_nt(kb[half:tk], qc[half:tq]) + mask_ref[half:tk, half:tq]
                s_ref[slot, c, 0:half, :] = upper
                s_ref[slot, c, half:tk, half:tq] = lower
                left = jnp.max(upper[:, 0:half], axis=0, keepdims=True)
                right = jnp.maximum(jnp.max(upper[:, half:tq], axis=0, keepdims=True),
                                    jnp.max(lower, axis=0, keepdims=True))
                tops.append(jnp.concatenate([left, right], axis=1))
            else:
                s = _add_key_bias(_dot_nt(kb, qc), bias_ref[...])
                s_ref[slot, c] = s
                tops.append(jnp.max(s, axis=0, keepdims=True) + block_offset(qi, j))
        return tops

    def exp_pv(slot, qi, j, tops, maxima):
        vb = jnp.concatenate([vt_ref[0, j], ones_rows], axis=0)
        new_maxima = []
        for c in range(2):
            if j == qi:
                m_new = tops[c]
                p_upper = jnp.exp2(s_ref[slot, c, 0:half, :] - m_new).astype(_BF16)
                p_lower = jnp.exp2(s_ref[slot, c, half:tk, half:tq] - m_new[:, half:tq]).astype(_BF16)
                from_upper = _dot(vb[:, 0:half], p_upper)
                from_lower = _dot(vb[:, half:tk], p_lower)
                acc = jnp.concatenate([from_upper[:, 0:half], from_upper[:, half:tq] + from_lower], axis=1)
            else:
                m_new = jnp.maximum(maxima[c], tops[c])
                alpha = jnp.exp2(maxima[c] - m_new)
                p = jnp.exp2(s_ref[slot, c] - (m_new - block_offset(qi, j))).astype(_BF16)
                acc = alpha * acc_ref[qi % 2, c] + _dot(vb, p)
            acc_ref[qi % 2, c] = acc
            new_maxima.append(m_new)
        return new_maxima

    def finish(qi):
        w = HEAD_WIDTH
        acc0, acc1 = acc_ref[qi % 2, 0], acc_ref[qi % 2, 1]
        o = acc0[0:w] / acc0[w:w + 1] - lam * (acc1[0:w] / acc1[w:w + 1])
        o = o * lax.rsqrt(jnp.mean(o * o, axis=0, keepdims=True) + NORM_EPS)
        o_ref[0, qi * tq:(qi + 1) * tq, :] = (o.T * sg_ref[...] * (1.0 - lambda_init)).astype(o_ref.dtype)

    pairs = [(qi, j) for qi in range(n_q) for j in [qi] + list(range(qi))]
    tops = scores(0, *pairs[0])
    maxima = None
    for idx, (qi, j) in enumerate(pairs):
        if idx + 1 < len(pairs):
            next_tops = scores((idx + 1) % 2, *pairs[idx + 1])
        maxima = exp_pv(idx % 2, qi, j, tops, maxima)
        if j == (qi - 1 if qi else 0):
            finish(qi)
        tops = next_tops


def _diff_attention(q, k, vt, slopes, lam_params, subln_g, lambda_init):
    b, s, d = q.shape
    n_kv = vt.shape[1]
    kern = functools.partial(_attn_kernel, lambda_init=lambda_init, tq=Q_TILE)
    head_rows = pl.BlockSpec((1, s, HEAD_WIDTH), lambda bi, hi: (bi, 0, hi))
    return pl.pallas_call(
        kern,
        grid=(b, N_HEADS),
        in_specs=[
            pl.BlockSpec(memory_space=pltpu.SMEM),
            head_rows,
            head_rows,
            pl.BlockSpec((1, n_kv, HEAD_WIDTH, KV_TILE), lambda bi, hi: (bi, 0, hi, 0)),
            pl.BlockSpec(lam_params.shape, lambda bi, hi: (0, 0)),
            pl.BlockSpec(subln_g.shape, lambda bi, hi: (0, 0)),
        ],
        out_specs=head_rows,
        out_shape=jax.ShapeDtypeStruct((b, s, d), _BF16),
        scratch_shapes=[pltpu.VMEM((2, s, HEAD_WIDTH), _BF16),
                        pltpu.VMEM((KV_TILE, LANES), _F32),
                        pltpu.VMEM((KV_TILE, Q_TILE), _F32),
                        pltpu.VMEM((2, 2, KV_TILE, Q_TILE), _F32),
                        pltpu.VMEM((2, 2, HEAD_WIDTH + SUM_ROWS, Q_TILE), _F32)],
        compiler_params=pltpu.CompilerParams(
            dimension_semantics=("parallel", "parallel"),
            vmem_limit_bytes=VMEM_LIMIT_BYTES),
        name="diff_attention",
    )(slopes, q, k, vt, lam_params, subln_g)


def _oproj_kernel(h_ref, o_ref, g_ref, wo_ref, out_ref):
    h = h_ref[...]
    out_ref[...] = h + _rms(_dot(o_ref[...], wo_ref[...]), g_ref[...])


def _o_proj(h, o, g, w_o):
    t, d = h.shape
    row = pl.BlockSpec((ROW_TILE, d), lambda i: (i, 0))
    return pl.pallas_call(
        _oproj_kernel,
        grid=(t // ROW_TILE,),
        in_specs=[row, row, _resident((1, d)), _resident(w_o.shape)],
        out_specs=row,
        out_shape=jax.ShapeDtypeStruct(h.shape, _F32),
        compiler_params=_row_params(("parallel",)),
        name="o_proj",
    )(h, o, g, w_o)


def kernel(x, ffn_w_gu, ffn_w_down, norm_g, conv_w_in, conv_k, conv_w_out, kv_norm_g, w_kv,
           attn_w_q, attn_lambda, attn_subln_g, attn_w_o):
    bsz, seq, d = x.shape
    assert d == D_MODEL and seq % ROW_TILE == 0 and seq % KV_TILE == 0 and Q_TILE == KV_TILE
    depth = ffn_w_gu.shape[0]
    assert depth == 2 * N_A_LAYERS

    bf = lambda w: w.astype(_BF16)
    g = lambda l, i: norm_g[l, i].reshape(1, d)
    h = x.reshape(bsz * seq, d)
    w_gu, w_down = ffn_w_gu, ffn_w_down

    h = _ffn(h, g(0, 0), g(0, 1), w_gu, w_down, 0, 0)
    h = _conv_mixer(h, g(0, 2), g(0, 3), bf(conv_w_in[0]), conv_k[0], bf(conv_w_out[0]), seq)
    h = _ffn(h, g(0, 4), g(0, 5), w_gu, w_down, 0, 1)

    k, vt = _kv_proj(h, kv_norm_g.reshape(1, d), bf(w_kv[:, :d]), bf(w_kv[:, d:].T), seq)

    h = _ffn(h, g(1, 0), g(1, 1), w_gu, w_down, 1, 0)
    q = _q_proj(h, g(1, 2), bf(attn_w_q[0]))
    lambda_init = 0.8 - 0.6 * math.exp(-0.3 * 1)
    slopes = 2.0 ** (-8.0 * jnp.arange(1, N_HEADS + 1, dtype=_F32) / N_HEADS)
    o = _diff_attention(q.reshape(bsz, seq, d), k.reshape(bsz, seq, d), vt, slopes,
                        attn_lambda[0], attn_subln_g[0].reshape(1, HEAD_WIDTH), lambda_init)
    h = _o_proj(h, o.reshape(bsz * seq, d), g(1, 3), bf(attn_w_o[0]))
    h = _ffn(h, g(1, 4), g(1, 5), w_gu, w_down, 1, 1)
    return h.reshape(bsz, seq, d)
```

```python
import functools
import math

import jax
import jax.numpy as jnp
from jax import lax
from jax.experimental import pallas as pl
from jax.experimental.pallas import tpu as pltpu

D_MODEL = 1024
D_FF = 2816
HEAD_DIM = 64
HEAD_WIDTH = 2 * HEAD_DIM
N_HEADS = D_MODEL // HEAD_WIDTH
CONV_WIDTH = 3
NORM_EPS = 1e-6
N_A_LAYERS = 1

LANES = 128
SUBLANES = 8
ROW_TILE = 1024
SUB_ROWS = 512
NORM_GROUPS = 8
FF_CHUNK = 256
CONV_CHUNK = 1024
CONV_HALO = 8
Q_TILE = 512
KV_TILE = 512
SUM_ROWS = 16
LOG2_E = math.log2(math.e)
VMEM_LIMIT_BYTES = 56 * 1024 * 1024

_F32 = jnp.float32
_BF16 = jnp.bfloat16


def _rms(x, g):
    return x * lax.rsqrt(jnp.mean(x * x, axis=-1, keepdims=True) + NORM_EPS) * g


def _dot(a, b):
    return jnp.dot(a, b, preferred_element_type=_F32)


def _dot_nt(a, b):
    return lax.dot_general(a, b, (((1,), (1,)), ((), ())), preferred_element_type=_F32)


def _resident(shape):
    return pl.BlockSpec(shape, lambda *_: (0,) * len(shape), pipeline_mode=pl.Buffered(1))


def _row_params(semantics):
    return pltpu.CompilerParams(dimension_semantics=semantics, vmem_limit_bytes=VMEM_LIMIT_BYTES)


def _max_tiles(x, rows):
    tile = None
    for i in range(0, x.shape[0], rows):
        for j in range(0, x.shape[1], LANES):
            part = x[i:i + rows, j:j + LANES]
            tile = part if tile is None else jnp.maximum(tile, part)
    return tile


def _after(x, anchor):
    rows = SUBLANES * 4 // x.dtype.itemsize
    tile = _max_tiles(anchor, rows).astype(x.dtype)
    other = jnp.tile(tile, (x.shape[0] // rows, x.shape[1] // LANES))
    return jnp.where(pl.program_id(0) >= 0, x, other)


def _ffn_kernel(h_ref, gpre_ref, gpost_ref, wgu_ref, wd_ref, o_ref, xn_ref, acc_ref):
    n_sub = h_ref.shape[0] // SUB_ROWS
    group = SUB_ROWS // NORM_GROUPS

    def rows(r, g):
        return slice(r * SUB_ROWS + g * group, r * SUB_ROWS + (g + 1) * group)

    def pre(r):
        sub = slice(r * SUB_ROWS, (r + 1) * SUB_ROWS)
        xn_ref[r] = _rms(h_ref[sub, :], gpre_ref[...]).astype(_BF16)

    def post(r, g, anchor=None):
        acc = acc_ref[r, g * group:(g + 1) * group, :]
        if anchor is not None:
            acc = _after(acc, anchor)
        o_ref[rows(r, g), :] = h_ref[rows(r, g), :] + 0.5 * _rms(acc, gpost_ref[...])

    pre(0)
    for r in range(n_sub):
        if r + 1 < n_sub:
            pre(r + 1)
        xn = xn_ref[r]
        acc = jnp.zeros((SUB_ROWS, h_ref.shape[1]), _F32)
        for c in range(D_FF // FF_CHUNK):
            lo = c * FF_CHUNK
            gate = _dot(xn, wgu_ref[:, lo:lo + FF_CHUNK].astype(_BF16))
            up = _dot(xn, wgu_ref[:, D_FF + lo:D_FF + lo + FF_CHUNK].astype(_BF16))
            act = (jax.nn.silu(gate) * up).astype(_BF16)
            g = c - 1
            if 0 <= g < NORM_GROUPS:
                if r + 1 < n_sub:
                    act = _after(act, xn_ref[r + 1, g * group:(g + 1) * group, :])
                if r > 0:
                    post(r - 1, g, anchor=acc[0:SUBLANES, :])
            acc = acc + _dot(act, wd_ref[lo:lo + FF_CHUNK, :].astype(_BF16))
        acc_ref[r] = acc
    for g in range(NORM_GROUPS):
        post(n_sub - 1, g)


def _ffn(h, g_pre, g_post, w_gu, w_down, layer, half):
    t, d = h.shape
    row = pl.BlockSpec((ROW_TILE, d), lambda i: (i, 0))

    def picked(w):
        return pl.BlockSpec((None, None) + w.shape[2:], lambda i: (layer, half, 0, 0),
                            pipeline_mode=pl.Buffered(1))

    return pl.pallas_call(
        _ffn_kernel,
        grid=(t // ROW_TILE,),
        in_specs=[row, _resident((1, d)), _resident((1, d)), picked(w_gu), picked(w_down)],
        out_specs=row,
        out_shape=jax.ShapeDtypeStruct(h.shape, _F32),
        scratch_shapes=[pltpu.VMEM((ROW_TILE // SUB_ROWS, SUB_ROWS, d), _BF16),
                        pltpu.VMEM((ROW_TILE // SUB_ROWS, SUB_ROWS, d), _F32)],
        compiler_params=_row_params(("parallel",)),
        name="ffn_half",
    )(h, g_pre, g_post, w_gu, w_down)


def _conv_kernel(h_ref, gpre_ref, gpost_ref, win_ref, ck_ref, wout_ref, o_ref, u_ref, *, tiles_per_seq):
    tm, d = h_ref.shape
    first_of_seq = pl.program_id(0) % tiles_per_seq == 0

    @pl.when(first_of_seq)
    def _():
        u_ref[0:CONV_HALO, :] = jnp.zeros((CONV_HALO, d), _F32)

    @pl.when(jnp.logical_not(first_of_seq))
    def _():
        u_ref[0:CONV_HALO, :] = u_ref[tm:tm + CONV_HALO, :]

    h = h_ref[...]
    xn = _rms(h, gpre_ref[...]).astype(_BF16)
    acc = jnp.zeros(h.shape, _F32)
    for c in range(d // CONV_CHUNK):
        lo = c * CONV_CHUNK
        cols = slice(lo, lo + CONV_CHUNK)
        b_gate = _dot(xn, win_ref[:, lo:lo + CONV_CHUNK])
        c_gate = _dot(xn, win_ref[:, d + lo:d + lo + CONV_CHUNK])
        z = _dot(xn, win_ref[:, 2 * d + lo:2 * d + lo + CONV_CHUNK])
        u = c_gate * z
        u_ref[CONV_HALO:CONV_HALO + tm, cols] = u
        conv = (ck_ref[2:3, cols] * u
                + ck_ref[1:2, cols] * u_ref[CONV_HALO - 1:CONV_HALO - 1 + tm, cols]
                + ck_ref[0:1, cols] * u_ref[CONV_HALO - 2:CONV_HALO - 2 + tm, cols])
        gated = (b_gate * conv).astype(_BF16)
        acc = acc + _dot(gated, wout_ref[lo:lo + CONV_CHUNK, :])
    o_ref[...] = h + _rms(acc, gpost_ref[...])


def _conv_mixer(h, g_pre, g_post, w_in, conv_k, w_out, seq_len):
    t, d = h.shape
    row = pl.BlockSpec((ROW_TILE, d), lambda i: (i, 0))
    kern = functools.partial(_conv_kernel, tiles_per_seq=seq_len // ROW_TILE)
    return pl.pallas_call(
        kern,
        grid=(t // ROW_TILE,),
        in_specs=[row, _resident((1, d)), _resident((1, d)), _resident(w_in.shape),
                  _resident(conv_k.shape), _resident(w_out.shape)],
        out_specs=row,
        out_shape=jax.ShapeDtypeStruct(h.shape, _F32),
        scratch_shapes=[pltpu.VMEM((ROW_TILE + CONV_HALO, d), _F32)],
        compiler_params=_row_params(("arbitrary",)),
        name="conv_mixer",
    )(h, g_pre, g_post, w_in, conv_k, w_out)


def _kv_kernel(h_ref, g_ref, wk_ref, wvt_ref, k_ref, vt_ref):
    xn = _rms(h_ref[...], g_ref[...]).astype(_BF16)
    k_ref[...] = _dot(xn, wk_ref[...]).astype(_BF16)
    for blk in range(vt_ref.shape[1]):
        vt_ref[0, blk] = _dot_nt(wvt_ref[...], xn[blk * KV_TILE:(blk + 1) * KV_TILE]).astype(_BF16)


def _kv_proj(h, g, w_k, w_vt, seq_len):
    t, d = h.shape
    n_kv = seq_len // KV_TILE
    per_step = ROW_TILE // KV_TILE
    steps_per_seq = seq_len // ROW_TILE
    row = pl.BlockSpec((ROW_TILE, d), lambda i: (i, 0))
    return pl.pallas_call(
        _kv_kernel,
        grid=(t // ROW_TILE,),
        in_specs=[row, _resident((1, d)), _resident(w_k.shape), _resident(w_vt.shape)],
        out_specs=[row, pl.BlockSpec((1, per_step, d, KV_TILE),
                                     lambda i: (i // steps_per_seq, i % steps_per_seq, 0, 0))],
        out_shape=[jax.ShapeDtypeStruct((t, d), _BF16),
                   jax.ShapeDtypeStruct((t // seq_len, n_kv, d, KV_TILE), _BF16)],
        compiler_params=_row_params(("parallel",)),
        name="kv_proj",
    )(h, g, w_k, w_vt)


def _q_kernel(h_ref, g_ref, wq_ref, q_ref):
    xn = _rms(h_ref[...], g_ref[...]).astype(_BF16)
    q_ref[...] = (_dot(xn, wq_ref[...]) * (HEAD_DIM ** -0.5 * LOG2_E)).astype(_BF16)


def _q_proj(h, g, w_q):
    t, d = h.shape
    row = pl.BlockSpec((ROW_TILE, d), lambda i: (i, 0))
    return pl.pallas_call(
        _q_kernel,
        grid=(t // ROW_TILE,),
        in_specs=[row, _resident((1, d)), _resident(w_q.shape)],
        out_specs=row,
        out_shape=jax.ShapeDtypeStruct((t, d), _BF16),
        compiler_params=_row_params(("parallel",)),
        name="q_proj",
    )(h, g, w_q)


def _add_key_bias(s, bias):
    return jnp.concatenate([s[:, lo:lo + LANES] + bias for lo in range(0, s.shape[1], LANES)], axis=1)


def _attn_kernel(slopes_ref, q_ref, k_ref, vt_ref, lam_ref, sg_ref, o_ref,
                 qm_ref, bias_ref, mask_ref, s_ref, acc_ref, *, lambda_init, tq):
    seq = q_ref.shape[1]
    tk = vt_ref.shape[3]
    assert tq == tk and tk % 2 == 0
    half = tk // 2
    n_q = seq // tq
    slope = slopes_ref[pl.program_id(1)] * LOG2_E

    bias_ref[...] = slope * lax.broadcasted_iota(jnp.int32, bias_ref.shape, 0).astype(_F32)
    kk = lax.broadcasted_iota(jnp.int32, (tk, tq), 0)
    qq = lax.broadcasted_iota(jnp.int32, (tk, tq), 1)
    mask_ref[...] = jnp.where(kk <= qq, slope * kk.astype(_F32), -jnp.inf)

    q = q_ref[0]
    lane = lax.broadcasted_iota(jnp.int32, q.shape, 1)
    zero = jnp.zeros_like(q)
    qm_ref[0] = jnp.where(lane < HEAD_DIM, q, zero)
    qm_ref[1] = jnp.where(lane >= HEAD_DIM, q, zero)

    ones_rows = jnp.ones((SUM_ROWS, tk), _BF16)

    lp = lam_ref[...]
    lam = (jnp.exp(jnp.sum(lp[0:1] * lp[1:2], axis=-1, keepdims=True))
           - jnp.exp(jnp.sum(lp[2:3] * lp[3:4], axis=-1, keepdims=True)) + lambda_init)

    def block_offset(qi, j):
        return slope * float(j * tk - qi * tq)

    def scores(slot, qi, j):
        kb = k_ref[0, j * tk:(j + 1) * tk, :]
        tops = []
        for c in range(2):
            qc = qm_ref[c, qi * tq:(qi + 1) * tq, :]
            if j == qi:
                upper = _dot_nt(kb[0:half], qc) + mask_ref[0:half, :]
                lower = _dot_nt(kb[half:tk], qc[half:tq]) + mask_ref[half:tk, half:tq]
                s_ref[slot, c, 0:half, :] = upper
                s_ref[slot, c, half:tk, half:tq] = lower
                left = jnp.max(upper[:, 0:half], axis=0, keepdims=True)
                right = jnp.maximum(jnp.max(upper[:, half:tq], axis=0, keepdims=True),
                                    jnp.max(lower, axis=0, keepdims=True))
                tops.append(jnp.concatenate([left, right], axis=1))
            else:
                s = _add_key_bias(_dot_nt(kb, qc), bias_ref[...])
                s_ref[slot, c] = s
                tops.append(jnp.max(s, axis=0, keepdims=True) + block_offset(qi, j))
        return tops

    def exp_pv(slot, qi, j, tops, maxima):
        vb = jnp.concatenate([vt_ref[0, j], ones_rows], axis=0)
        new_maxima = []
        for c in range(2):
            if j == qi:
                m_new = tops[c]
                p_upper = jnp.exp2(s_ref[slot, c, 0:half, :] - m_new).astype(_BF16)
                p_lower = jnp.exp2(s_ref[slot, c, half:tk, half:tq] - m_new[:, half:tq]).astype(_BF16)
                from_upper = _dot(vb[:, 0:half], p_upper)
                from_lower = _dot(vb[:, half:tk], p_lower)
                acc = jnp.concatenate([from_upper[:, 0:half], from_upper[:, half:tq] + from_lower], axis=1)
            else:
                m_new = jnp.maximum(maxima[c], tops[c])
                alpha = jnp.exp2(maxima[c] - m_new)
                p = jnp.exp2(s_ref[slot, c] - (m_new - block_offset(qi, j))).astype(_BF16)
                acc = alpha * acc_ref[qi % 2, c] + _dot(vb, p)
            acc_ref[qi % 2, c] = acc
            new_maxima.append(m_new)
        return new_maxima

    def finish(qi):
        w = HEAD_WIDTH
        acc0, acc1 = acc_ref[qi % 2, 0], acc_ref[qi % 2, 1]
        o = acc0[0:w] / acc0[w:w + 1] - lam * (acc1[0:w] / acc1[w:w + 1])
        o = o * lax.rsqrt(jnp.mean(o * o, axis=0, keepdims=True) + NORM_EPS)
        o_ref[0, qi * tq:(qi + 1) * tq, :] = (o.T * sg_ref[...] * (1.0 - lambda_init)).astype(o_ref.dtype)

    pairs = [(qi, j) for qi in range(n_q) for j in [qi] + list(range(qi))]
    tops = scores(0, *pairs[0])
    maxima = None
    for idx, (qi, j) in enumerate(pairs):
        if idx + 1 < len(pairs):
            next_tops = scores((idx + 1) % 2, *pairs[idx + 1])
        maxima = exp_pv(idx % 2, qi, j, tops, maxima)
        if j == (qi - 1 if qi else 0):
            finish(qi)
        tops = next_tops


def _diff_attention(q, k, vt, slopes, lam_params, subln_g, lambda_init):
    b, s, d = q.shape
    n_kv = vt.shape[1]
    kern = functools.partial(_attn_kernel, lambda_init=lambda_init, tq=Q_TILE)
    head_rows = pl.BlockSpec((1, s, HEAD_WIDTH), lambda bi, hi: (bi, 0, hi))
    return pl.pallas_call(
        kern,
        grid=(b, N_HEADS),
        in_specs=[
            pl.BlockSpec(memory_space=pltpu.SMEM),
            head_rows,
            head_rows,
            pl.BlockSpec((1, n_kv, HEAD_WIDTH, KV_TILE), lambda bi, hi: (bi, 0, hi, 0)),
            pl.BlockSpec(lam_params.shape, lambda bi, hi: (0, 0)),
            pl.BlockSpec(subln_g.shape, lambda bi, hi: (0, 0)),
        ],
        out_specs=head_rows,
        out_shape=jax.ShapeDtypeStruct((b, s, d), _BF16),
        scratch_shapes=[pltpu.VMEM((2, s, HEAD_WIDTH), _BF16),
                        pltpu.VMEM((KV_TILE, LANES), _F32),
                        pltpu.VMEM((KV_TILE, Q_TILE), _F32),
                        pltpu.VMEM((2, 2, KV_TILE, Q_TILE), _F32),
                        pltpu.VMEM((2, 2, HEAD_WIDTH + SUM_ROWS, Q_TILE), _F32)],
        compiler_params=pltpu.CompilerParams(
            dimension_semantics=("parallel", "parallel"),
            vmem_limit_bytes=VMEM_LIMIT_BYTES),
        name="diff_attention",
    )(slopes, q, k, vt, lam_params, subln_g)


def _oproj_kernel(h_ref, o_ref, g_ref, wo_ref, out_ref):
    h = h_ref[...]
    out_ref[...] = h + _rms(_dot(o_ref[...], wo_ref[...]), g_ref[...])


def _o_proj(h, o, g, w_o):
    t, d = h.shape
    row = pl.BlockSpec((ROW_TILE, d), lambda i: (i, 0))
    return pl.pallas_call(
        _oproj_kernel,
        grid=(t // ROW_TILE,),
        in_specs=[row, row, _resident((1, d)), _resident(w_o.shape)],
        out_specs=row,
        out_shape=jax.ShapeDtypeStruct(h.shape, _F32),
        compiler_params=_row_params(("parallel",)),
        name="o_proj",
    )(h, o, g, w_o)


def kernel(x, ffn_w_gu, ffn_w_down, norm_g, conv_w_in, conv_k, conv_w_out, kv_norm_g, w_kv,
           attn_w_q, attn_lambda, attn_subln_g, attn_w_o):
    bsz, seq, d = x.shape
    assert d == D_MODEL and seq % ROW_TILE == 0 and seq % KV_TILE == 0 and Q_TILE == KV_TILE
    depth = ffn_w_gu.shape[0]
    assert depth == 2 * N_A_LAYERS

    bf = lambda w: w.astype(_BF16)
    g = lambda l, i: norm_g[l, i].reshape(1, d)
    h = x.reshape(bsz * seq, d)
    w_gu, w_down = ffn_w_gu, ffn_w_down

    h = _ffn(h, g(0, 0), g(0, 1), w_gu, w_down, 0, 0)
    h = _conv_mixer(h, g(0, 2), g(0, 3), bf(conv_w_in[0]), conv_k[0], bf(conv_w_out[0]), seq)
    h = _ffn(h, g(0, 4), g(0, 5), w_gu, w_down, 0, 1)

    k, vt = _kv_proj(h, kv_norm_g.reshape(1, d), bf(w_kv[:, :d]), bf(w_kv[:, d:].T), seq)

    h = _ffn(h, g(1, 0), g(1, 1), w_gu, w_down, 1, 0)
    q = _q_proj(h, g(1, 2), bf(attn_w_q[0]))
    lambda_init = 0.8 - 0.6 * math.exp(-0.3 * 1)
    slopes = 2.0 ** (-8.0 * jnp.arange(1, N_HEADS + 1, dtype=_F32) / N_HEADS)
    o = _diff_attention(q.reshape(bsz, seq, d), k.reshape(bsz, seq, d), vt, slopes,
                        attn_lambda[0], attn_subln_g[0].reshape(1, HEAD_WIDTH), lambda_init)
    h = _o_proj(h, o.reshape(bsz * seq, d), g(1, 3), bf(attn_w_o[0]))
    h = _ffn(h, g(1, 4), g(1, 5), w_gu, w_down, 1, 1)
    return h.reshape(bsz, seq, d)
```

```python
import functools
import math

import jax
import jax.numpy as jnp
from jax import lax
from jax.experimental import pallas as pl
from jax.experimental.pallas import tpu as pltpu

D_MODEL = 1024
D_FF = 2816
HEAD_DIM = 64
HEAD_WIDTH = 2 * HEAD_DIM
N_HEADS = D_MODEL // HEAD_WIDTH
CONV_WIDTH = 3
NORM_EPS = 1e-6
N_A_LAYERS = 1

LANES = 128
SUBLANES = 8
ROW_TILE = 1024
SUB_ROWS = 512
NORM_GROUPS = 8
FF_CHUNK = 256
CONV_CHUNK = 1024
CONV_HALO = 8
Q_TILE = 512
KV_TILE = 512
SUM_ROWS = 16
LOG2_E = math.log2(math.e)
VMEM_LIMIT_BYTES = 56 * 1024 * 1024

_F32 = jnp.float32
_BF16 = jnp.bfloat16


def _rms(x, g):
    return x * lax.rsqrt(jnp.mean(x * x, axis=-1, keepdims=True) + NORM_EPS) * g


def _dot(a, b):
    return jnp.dot(a, b, preferred_element_type=_F32)


def _dot_nt(a, b):
    return lax.dot_general(a, b, (((1,), (1,)), ((), ())), preferred_element_type=_F32)


def _resident(shape):
    return pl.BlockSpec(shape, lambda *_: (0,) * len(shape), pipeline_mode=pl.Buffered(1))


def _row_params(semantics):
    return pltpu.CompilerParams(dimension_semantics=semantics, vmem_limit_bytes=VMEM_LIMIT_BYTES)


def _max_tiles(x, rows):
    tile = None
    for i in range(0, x.shape[0], rows):
        for j in range(0, x.shape[1], LANES):
            part = x[i:i + rows, j:j + LANES]
            tile = part if tile is None else jnp.maximum(tile, part)
    return tile


def _after(x, anchor):
    rows = SUBLANES * 4 // x.dtype.itemsize
    tile = _max_tiles(anchor, rows).astype(x.dtype)
    other = jnp.tile(tile, (x.shape[0] // rows, x.shape[1] // LANES))
    return jnp.where(pl.program_id(0) >= 0, x, other)


def _ffn_kernel(h_ref, gpre_ref, gpost_ref, wgu_ref, wd_ref, o_ref, xn_ref, acc_ref):
    n_sub = h_ref.shape[0] // SUB_ROWS
    group = SUB_ROWS // NORM_GROUPS

    def rows(r, g):
        return slice(r * SUB_ROWS + g * group, r * SUB_ROWS + (g + 1) * group)

    def pre(r):
        sub = slice(r * SUB_ROWS, (r + 1) * SUB_ROWS)
        xn_ref[r] = _rms(h_ref[sub, :], gpre_ref[...]).astype(_BF16)

    def post(r, g, anchor=None):
        acc = acc_ref[r, g * group:(g + 1) * group, :]
        if anchor is not None:
            acc = _after(acc, anchor)
        o_ref[rows(r, g), :] = h_ref[rows(r, g), :] + 0.5 * _rms(acc, gpost_ref[...])

    pre(0)
    for r in range(n_sub):
        if r + 1 < n_sub:
            pre(r + 1)
        xn = xn_ref[r]
        acc = jnp.zeros((SUB_ROWS, h_ref.shape[1]), _F32)
        for c in range(D_FF // FF_CHUNK):
            lo = c * FF_CHUNK
            gate = _dot(xn, wgu_ref[:, lo:lo + FF_CHUNK].astype(_BF16))
            up = _dot(xn, wgu_ref[:, D_FF + lo:D_FF + lo + FF_CHUNK].astype(_BF16))
            act = (jax.nn.silu(gate) * up).astype(_BF16)
            g = c - 1
            if 0 <= g < NORM_GROUPS:
                if r + 1 < n_sub:
                    act = _after(act, xn_ref[r + 1, g * group:(g + 1) * group, :])
                if r > 0:
                    post(r - 1, g, anchor=acc[0:SUBLANES, :])
            acc = acc + _dot(act, wd_ref[lo:lo + FF_CHUNK, :].astype(_BF16))
        acc_ref[r] = acc
    for g in range(NORM_GROUPS):
        post(n_sub - 1, g)


def _ffn(h, g_pre, g_post, w_gu, w_down, layer, half):
    t, d = h.shape
    row = pl.BlockSpec((ROW_TILE, d), lambda i: (i, 0))

    def picked(w):
        return pl.BlockSpec((None, None) + w.shape[2:], lambda i: (layer, half, 0, 0),
                            pipeline_mode=pl.Buffered(1))

    return pl.pallas_call(
        _ffn_kernel,
        grid=(t // ROW_TILE,),
        in_specs=[row, _resident((1, d)), _resident((1, d)), picked(w_gu), picked(w_down)],
        out_specs=row,
        out_shape=jax.ShapeDtypeStruct(h.shape, _F32),
        scratch_shapes=[pltpu.VMEM((ROW_TILE // SUB_ROWS, SUB_ROWS, d), _BF16),
                        pltpu.VMEM((ROW_TILE // SUB_ROWS, SUB_ROWS, d), _F32)],
        compiler_params=_row_params(("parallel",)),
        name="ffn_half",
    )(h, g_pre, g_post, w_gu, w_down)


def _conv_kernel(h_ref, gpre_ref, gpost_ref, win_ref, ck_ref, wout_ref, o_ref, u_ref, *, tiles_per_seq):
    tm, d = h_ref.shape
    first_of_seq = pl.program_id(0) % tiles_per_seq == 0

    @pl.when(first_of_seq)
    def _():
        u_ref[0:CONV_HALO, :] = jnp.zeros((CONV_HALO, d), _F32)

    @pl.when(jnp.logical_not(first_of_seq))
    def _():
        u_ref[0:CONV_HALO, :] = u_ref[tm:tm + CONV_HALO, :]

    h = h_ref[...]
    xn = _rms(h, gpre_ref[...]).astype(_BF16)
    acc = jnp.zeros(h.shape, _F32)
    for c in range(d // CONV_CHUNK):
        lo = c * CONV_CHUNK
        cols = slice(lo, lo + CONV_CHUNK)
        b_gate = _dot(xn, win_ref[:, lo:lo + CONV_CHUNK].astype(_BF16))
        c_gate = _dot(xn, win_ref[:, d + lo:d + lo + CONV_CHUNK].astype(_BF16))
        z = _dot(xn, win_ref[:, 2 * d + lo:2 * d + lo + CONV_CHUNK].astype(_BF16))
        u = c_gate * z
        u_ref[CONV_HALO:CONV_HALO + tm, cols] = u
        conv = (ck_ref[2:3, cols] * u
                + ck_ref[1:2, cols] * u_ref[CONV_HALO - 1:CONV_HALO - 1 + tm, cols]
                + ck_ref[0:1, cols] * u_ref[CONV_HALO - 2:CONV_HALO - 2 + tm, cols])
        gated = (b_gate * conv).astype(_BF16)
        acc = acc + _dot(gated, wout_ref[lo:lo + CONV_CHUNK, :].astype(_BF16))
    o_ref[...] = h + _rms(acc, gpost_ref[...])


def _conv_mixer(h, g_pre, g_post, w_in, conv_k, w_out, seq_len):
    t, d = h.shape
    row = pl.BlockSpec((ROW_TILE, d), lambda i: (i, 0))
    kern = functools.partial(_conv_kernel, tiles_per_seq=seq_len // ROW_TILE)
    return pl.pallas_call(
        kern,
        grid=(t // ROW_TILE,),
        in_specs=[row, _resident((1, d)), _resident((1, d)), _resident(w_in.shape),
                  _resident(conv_k.shape), _resident(w_out.shape)],
        out_specs=row,
        out_shape=jax.ShapeDtypeStruct(h.shape, _F32),
        scratch_shapes=[pltpu.VMEM((ROW_TILE + CONV_HALO, d), _F32)],
        compiler_params=_row_params(("arbitrary",)),
        name="conv_mixer",
    )(h, g_pre, g_post, w_in, conv_k, w_out)


def _kv_kernel(h_ref, g_ref, wk_ref, wvt_ref, k_ref, vt_ref):
    xn = _rms(h_ref[...], g_ref[...]).astype(_BF16)
    k_ref[...] = _dot(xn, wk_ref[...].astype(_BF16)).astype(_BF16)
    for blk in range(vt_ref.shape[1]):
        vt_ref[0, blk] = _dot_nt(wvt_ref[...], xn[blk * KV_TILE:(blk + 1) * KV_TILE]).astype(_BF16)


def _kv_proj(h, g, w_kv, w_vt, seq_len):
    t, d = h.shape
    n_kv = seq_len // KV_TILE
    per_step = ROW_TILE // KV_TILE
    steps_per_seq = seq_len // ROW_TILE
    row = pl.BlockSpec((ROW_TILE, d), lambda i: (i, 0))
    key_half = pl.BlockSpec((d, d), lambda i: (0, 0), pipeline_mode=pl.Buffered(1))
    return pl.pallas_call(
        _kv_kernel,
        grid=(t // ROW_TILE,),
        in_specs=[row, _resident((1, d)), key_half, _resident(w_vt.shape)],
        out_specs=[row, pl.BlockSpec((1, per_step, d, KV_TILE),
                                     lambda i: (i // steps_per_seq, i % steps_per_seq, 0, 0))],
        out_shape=[jax.ShapeDtypeStruct((t, d), _BF16),
                   jax.ShapeDtypeStruct((t // seq_len, n_kv, d, KV_TILE), _BF16)],
        compiler_params=_row_params(("parallel",)),
        name="kv_proj",
    )(h, g, w_kv, w_vt)


def _q_kernel(h_ref, g_ref, wq_ref, q_ref):
    xn = _rms(h_ref[...], g_ref[...]).astype(_BF16)
    q_ref[...] = (_dot(xn, wq_ref[...].astype(_BF16)) * (HEAD_DIM ** -0.5 * LOG2_E)).astype(_BF16)


def _q_proj(h, g, w_q):
    t, d = h.shape
    row = pl.BlockSpec((ROW_TILE, d), lambda i: (i, 0))
    return pl.pallas_call(
        _q_kernel,
        grid=(t // ROW_TILE,),
        in_specs=[row, _resident((1, d)), _resident(w_q.shape)],
        out_specs=row,
        out_shape=jax.ShapeDtypeStruct((t, d), _BF16),
        compiler_params=_row_params(("parallel",)),
        name="q_proj",
    )(h, g, w_q)


def _add_key_bias(s, bias):
    return jnp.concatenate([s[:, lo:lo + LANES] + bias for lo in range(0, s.shape[1], LANES)], axis=1)


def _attn_kernel(slopes_ref, q_ref, k_ref, vt_ref, lam_ref, sg_ref, o_ref,
                 qm_ref, bias_ref, mask_ref, s_ref, acc_ref, *, lambda_init, tq):
    seq = q_ref.shape[1]
    tk = vt_ref.shape[3]
    assert tq == tk and tk % 2 == 0
    half = tk // 2
    n_q = seq // tq
    slope = slopes_ref[pl.program_id(0)] * LOG2_E

    @pl.when(pl.program_id(1) == 0)
    def _():
        bias_ref[...] = slope * lax.broadcasted_iota(jnp.int32, bias_ref.shape, 0).astype(_F32)
        kk = lax.broadcasted_iota(jnp.int32, (tk, tq), 0)
        qq = lax.broadcasted_iota(jnp.int32, (tk, tq), 1)
        mask_ref[...] = jnp.where(kk <= qq, slope * kk.astype(_F32), -jnp.inf)

    q = q_ref[0]
    lane = lax.broadcasted_iota(jnp.int32, q.shape, 1)
    zero = jnp.zeros_like(q)
    qm_ref[0] = jnp.where(lane < HEAD_DIM, q, zero)
    qm_ref[1] = jnp.where(lane >= HEAD_DIM, q, zero)

    ones_rows = jnp.ones((SUM_ROWS, tk), _BF16)

    lp = lam_ref[...]
    lam = (jnp.exp(jnp.sum(lp[0:1] * lp[1:2], axis=-1, keepdims=True))
           - jnp.exp(jnp.sum(lp[2:3] * lp[3:4], axis=-1, keepdims=True)) + lambda_init)

    def block_offset(qi, j):
        return slope * float(j * tk - qi * tq)

    def scores(slot, qi, j):
        kb = k_ref[0, j * tk:(j + 1) * tk, :]
        tops = []
        for c in range(2):
            qc = qm_ref[c, qi * tq:(qi + 1) * tq, :]
            if j == qi:
                upper = _dot_nt(kb[0:half], qc) + mask_ref[0:half, :]
                lower = _dot_nt(kb[half:tk], qc[half:tq]) + mask_ref[half:tk, half:tq]
                s_ref[slot, c, 0:half, :] = upper
                s_ref[slot, c, half:tk, half:tq] = lower
                left = jnp.max(upper[:, 0:half], axis=0, keepdims=True)
                right = jnp.maximum(jnp.max(upper[:, half:tq], axis=0, keepdims=True),
                                    jnp.max(lower, axis=0, keepdims=True))
                tops.append(jnp.concatenate([left, right], axis=1))
            else:
                s = _add_key_bias(_dot_nt(kb, qc), bias_ref[...])
                s_ref[slot, c] = s
                tops.append(jnp.max(s, axis=0, keepdims=True) + block_offset(qi, j))
        return tops

    def exp_pv(slot, qi, j, tops, maxima):
        vb = jnp.concatenate([vt_ref[0, j], ones_rows], axis=0)
        new_maxima = []
        for c in range(2):
            if j == qi:
                m_new = tops[c]
                p_upper = jnp.exp2(s_ref[slot, c, 0:half, :] - m_new).astype(_BF16)
                p_lower = jnp.exp2(s_ref[slot, c, half:tk, half:tq] - m_new[:, half:tq]).astype(_BF16)
                from_upper = _dot(vb[:, 0:half], p_upper)
                from_lower = _dot(vb[:, half:tk], p_lower)
                acc = jnp.concatenate([from_upper[:, 0:half], from_upper[:, half:tq] + from_lower], axis=1)
            else:
                m_new = jnp.maximum(maxima[c], tops[c])
                alpha = jnp.exp2(maxima[c] - m_new)
                p = jnp.exp2(s_ref[slot, c] - (m_new - block_offset(qi, j))).astype(_BF16)
                acc = alpha * acc_ref[qi % 2, c] + _dot(vb, p)
            acc_ref[qi % 2, c] = acc
            new_maxima.append(m_new)
        return new_maxima

    def finish(qi):
        w = HEAD_WIDTH
        acc0, acc1 = acc_ref[qi % 2, 0], acc_ref[qi % 2, 1]
        o = acc0[0:w] / acc0[w:w + 1] - lam * (acc1[0:w] / acc1[w:w + 1])
        o = o * lax.rsqrt(jnp.mean(o * o, axis=0, keepdims=True) + NORM_EPS)
        o_ref[0, qi * tq:(qi + 1) * tq, :] = (o.T * sg_ref[...] * (1.0 - lambda_init)).astype(o_ref.dtype)

    pairs = [(qi, j) for qi in range(n_q) for j in [qi] + list(range(qi))]
    tops = scores(0, *pairs[0])
    maxima = None
    for idx, (qi, j) in enumerate(pairs):
        if idx + 1 < len(pairs):
            next_tops = scores((idx + 1) % 2, *pairs[idx + 1])
        maxima = exp_pv(idx % 2, qi, j, tops, maxima)
        if j == (qi - 1 if qi else 0):
            finish(qi)
        tops = next_tops


def _diff_attention(q, k, vt, slopes, lam_params, subln_g, lambda_init):
    b, s, d = q.shape
    n_kv = vt.shape[1]
    kern = functools.partial(_attn_kernel, lambda_init=lambda_init, tq=Q_TILE)
    head_rows = pl.BlockSpec((1, s, HEAD_WIDTH), lambda hi, bi: (bi, 0, hi))
    return pl.pallas_call(
        kern,
        grid=(N_HEADS, b),
        in_specs=[
            pl.BlockSpec(memory_space=pltpu.SMEM),
            head_rows,
            head_rows,
            pl.BlockSpec((1, n_kv, HEAD_WIDTH, KV_TILE), lambda hi, bi: (bi, 0, hi, 0)),
            pl.BlockSpec(lam_params.shape, lambda hi, bi: (0, 0)),
            pl.BlockSpec(subln_g.shape, lambda hi, bi: (0, 0)),
        ],
        out_specs=head_rows,
        out_shape=jax.ShapeDtypeStruct((b, s, d), _BF16),
        scratch_shapes=[pltpu.VMEM((2, s, HEAD_WIDTH), _BF16),
                        pltpu.VMEM((KV_TILE, LANES), _F32),
                        pltpu.VMEM((KV_TILE, Q_TILE), _F32),
                        pltpu.VMEM((2, 2, KV_TILE, Q_TILE), _F32),
                        pltpu.VMEM((2, 2, HEAD_WIDTH + SUM_ROWS, Q_TILE), _F32)],
        compiler_params=pltpu.CompilerParams(
            dimension_semantics=("arbitrary", "arbitrary"),
            vmem_limit_bytes=VMEM_LIMIT_BYTES),
        name="diff_attention",
    )(slopes, q, k, vt, lam_params, subln_g)


def _oproj_kernel(h_ref, o_ref, g_ref, wo_ref, out_ref):
    h = h_ref[...]
    out_ref[...] = h + _rms(_dot(o_ref[...], wo_ref[...].astype(_BF16)), g_ref[...])


def _o_proj(h, o, g, w_o):
    t, d = h.shape
    row = pl.BlockSpec((ROW_TILE, d), lambda i: (i, 0))
    return pl.pallas_call(
        _oproj_kernel,
        grid=(t // ROW_TILE,),
        in_specs=[row, row, _resident((1, d)), _resident(w_o.shape)],
        out_specs=row,
        out_shape=jax.ShapeDtypeStruct(h.shape, _F32),
        compiler_params=_row_params(("parallel",)),
        name="o_proj",
    )(h, o, g, w_o)


def kernel(x, ffn_w_gu, ffn_w_down, norm_g, conv_w_in, conv_k, conv_w_out, kv_norm_g, w_kv,
           attn_w_q, attn_lambda, attn_subln_g, attn_w_o):
    bsz, seq, d = x.shape
    assert d == D_MODEL and seq % ROW_TILE == 0 and seq % KV_TILE == 0 and Q_TILE == KV_TILE
    depth = ffn_w_gu.shape[0]
    assert depth == 2 * N_A_LAYERS

    bf = lambda w: w.astype(_BF16)
    g = lambda l, i: norm_g[l, i].reshape(1, d)
    h = x.reshape(bsz * seq, d)
    w_gu, w_down = ffn_w_gu, ffn_w_down

    h = _ffn(h, g(0, 0), g(0, 1), w_gu, w_down, 0, 0)
    h = _conv_mixer(h, g(0, 2), g(0, 3), conv_w_in[0], conv_k[0], conv_w_out[0], seq)
    h = _ffn(h, g(0, 4), g(0, 5), w_gu, w_down, 0, 1)

    k, vt = _kv_proj(h, kv_norm_g.reshape(1, d), w_kv, bf(w_kv[:, d:].T), seq)

    h = _ffn(h, g(1, 0), g(1, 1), w_gu, w_down, 1, 0)
    q = _q_proj(h, g(1, 2), attn_w_q[0])
    lambda_init = 0.8 - 0.6 * math.exp(-0.3 * 1)
    slopes = 2.0 ** (-8.0 * jnp.arange(1, N_HEADS + 1, dtype=_F32) / N_HEADS)
    o = _diff_attention(q.reshape(bsz, seq, d), k.reshape(bsz, seq, d), vt, slopes,
                        attn_lambda[0], attn_subln_g[0].reshape(1, HEAD_WIDTH), lambda_init)
    h = _o_proj(h, o.reshape(bsz * seq, d), g(1, 3), attn_w_o[0])
    h = _ffn(h, g(1, 4), g(1, 5), w_gu, w_down, 1, 1)
    return h.reshape(bsz, seq, d)
```

```python
import functools
import math

import jax
import jax.numpy as jnp
from jax import lax
from jax.experimental import pallas as pl
from jax.experimental.pallas import tpu as pltpu

D_MODEL = 1024
D_FF = 2816
HEAD_DIM = 64
HEAD_WIDTH = 2 * HEAD_DIM
N_HEADS = D_MODEL // HEAD_WIDTH
CONV_WIDTH = 3
NORM_EPS = 1e-6
N_A_LAYERS = 1

LANES = 128
SUBLANES = 8
ROW_TILE = 1024
SUB_ROWS = 512
NORM_GROUPS = 8
FF_CHUNK = 256
CONV_CHUNK = 1024
CONV_HALO = 8
Q_TILE = 512
KV_TILE = 512
SUM_ROWS = 16
LOG2_E = math.log2(math.e)
VMEM_LIMIT_BYTES = 56 * 1024 * 1024

_F32 = jnp.float32
_BF16 = jnp.bfloat16


def _rms(x, g):
    return x * lax.rsqrt(jnp.mean(x * x, axis=-1, keepdims=True) + NORM_EPS) * g


def _dot(a, b):
    return jnp.dot(a, b, preferred_element_type=_F32)


def _dot_nt(a, b):
    return lax.dot_general(a, b, (((1,), (1,)), ((), ())), preferred_element_type=_F32)


def _resident(shape):
    return pl.BlockSpec(shape, lambda *_: (0,) * len(shape), pipeline_mode=pl.Buffered(1))


def _row_params(semantics):
    return pltpu.CompilerParams(dimension_semantics=semantics, vmem_limit_bytes=VMEM_LIMIT_BYTES)


def _max_tiles(x, rows):
    tile = None
    for i in range(0, x.shape[0], rows):
        for j in range(0, x.shape[1], LANES):
            part = x[i:i + rows, j:j + LANES]
            tile = part if tile is None else jnp.maximum(tile, part)
    return tile


def _after(x, anchor):
    rows = SUBLANES * 4 // x.dtype.itemsize
    tile = _max_tiles(anchor, rows).astype(x.dtype)
    other = jnp.tile(tile, (x.shape[0] // rows, x.shape[1] // LANES))
    return jnp.where(pl.program_id(0) >= 0, x, other)


def _ffn_kernel(h_ref, gpre_ref, gpost_ref, wgu_ref, wd_ref, o_ref, xn_ref, acc_ref):
    n_sub = h_ref.shape[0] // SUB_ROWS
    group = SUB_ROWS // NORM_GROUPS

    def rows(r, g):
        return slice(r * SUB_ROWS + g * group, r * SUB_ROWS + (g + 1) * group)

    def pre(r):
        sub = slice(r * SUB_ROWS, (r + 1) * SUB_ROWS)
        xn_ref[r] = _rms(h_ref[sub, :], gpre_ref[...]).astype(_BF16)

    def post(r, g, anchor=None):
        acc = acc_ref[r, g * group:(g + 1) * group, :]
        if anchor is not None:
            acc = _after(acc, anchor)
        o_ref[rows(r, g), :] = h_ref[rows(r, g), :] + 0.5 * _rms(acc, gpost_ref[...])

    pre(0)
    for r in range(n_sub):
        if r + 1 < n_sub:
            pre(r + 1)
        xn = xn_ref[r]
        acc = jnp.zeros((SUB_ROWS, h_ref.shape[1]), _F32)
        for c in range(D_FF // FF_CHUNK):
            lo = c * FF_CHUNK
            gate = _dot(xn, wgu_ref[:, lo:lo + FF_CHUNK].astype(_BF16))
            up = _dot(xn, wgu_ref[:, D_FF + lo:D_FF + lo + FF_CHUNK].astype(_BF16))
            act = (jax.nn.silu(gate) * up).astype(_BF16)
            g = c - 1
            if 0 <= g < NORM_GROUPS:
                if r + 1 < n_sub:
                    act = _after(act, xn_ref[r + 1, g * group:(g + 1) * group, :])
                if r > 0:
                    post(r - 1, g, anchor=acc[0:SUBLANES, :])
            acc = acc + _dot(act, wd_ref[lo:lo + FF_CHUNK, :].astype(_BF16))
        acc_ref[r] = acc
    for g in range(NORM_GROUPS):
        post(n_sub - 1, g)


def _ffn(h, g_pre, g_post, w_gu, w_down, layer, half):
    t, d = h.shape
    row = pl.BlockSpec((ROW_TILE, d), lambda i: (i, 0))

    def picked(w):
        return pl.BlockSpec((None, None) + w.shape[2:], lambda i: (layer, half, 0, 0),
                            pipeline_mode=pl.Buffered(1))

    return pl.pallas_call(
        _ffn_kernel,
        grid=(t // ROW_TILE,),
        in_specs=[row, _resident((1, d)), _resident((1, d)), picked(w_gu), picked(w_down)],
        out_specs=row,
        out_shape=jax.ShapeDtypeStruct(h.shape, _F32),
        scratch_shapes=[pltpu.VMEM((ROW_TILE // SUB_ROWS, SUB_ROWS, d), _BF16),
                        pltpu.VMEM((ROW_TILE // SUB_ROWS, SUB_ROWS, d), _F32)],
        compiler_params=_row_params(("parallel",)),
        name="ffn_half",
    )(h, g_pre, g_post, w_gu, w_down)


def _conv_kernel(h_ref, gpre_ref, gpost_ref, win_ref, ck_ref, wout_ref, o_ref, u_ref, *, tiles_per_seq):
    tm, d = h_ref.shape
    first_of_seq = pl.program_id(0) % tiles_per_seq == 0

    @pl.when(first_of_seq)
    def _():
        u_ref[0:CONV_HALO, :] = jnp.zeros((CONV_HALO, d), _F32)

    @pl.when(jnp.logical_not(first_of_seq))
    def _():
        u_ref[0:CONV_HALO, :] = u_ref[tm:tm + CONV_HALO, :]

    h = h_ref[...]
    xn = _rms(h, gpre_ref[...]).astype(_BF16)
    acc = jnp.zeros(h.shape, _F32)
    for c in range(d // CONV_CHUNK):
        lo = c * CONV_CHUNK
        cols = slice(lo, lo + CONV_CHUNK)
        b_gate = _dot(xn, win_ref[:, lo:lo + CONV_CHUNK].astype(_BF16))
        c_gate = _dot(xn, win_ref[:, d + lo:d + lo + CONV_CHUNK].astype(_BF16))
        z = _dot(xn, win_ref[:, 2 * d + lo:2 * d + lo + CONV_CHUNK].astype(_BF16))
        u = c_gate * z
        u_ref[CONV_HALO:CONV_HALO + tm, cols] = u
        conv = (ck_ref[2:3, cols] * u
                + ck_ref[1:2, cols] * u_ref[CONV_HALO - 1:CONV_HALO - 1 + tm, cols]
                + ck_ref[0:1, cols] * u_ref[CONV_HALO - 2:CONV_HALO - 2 + tm, cols])
        gated = (b_gate * conv).astype(_BF16)
        acc = acc + _dot(gated, wout_ref[lo:lo + CONV_CHUNK, :].astype(_BF16))
    o_ref[...] = h + _rms(acc, gpost_ref[...])


def _conv_mixer(h, g_pre, g_post, w_in, conv_k, w_out, seq_len):
    t, d = h.shape
    row = pl.BlockSpec((ROW_TILE, d), lambda i: (i, 0))
    kern = functools.partial(_conv_kernel, tiles_per_seq=seq_len // ROW_TILE)
    return pl.pallas_call(
        kern,
        grid=(t // ROW_TILE,),
        in_specs=[row, _resident((1, d)), _resident((1, d)), _resident(w_in.shape),
                  _resident(conv_k.shape), _resident(w_out.shape)],
        out_specs=row,
        out_shape=jax.ShapeDtypeStruct(h.shape, _F32),
        scratch_shapes=[pltpu.VMEM((ROW_TILE + CONV_HALO, d), _F32)],
        compiler_params=_row_params(("arbitrary",)),
        name="conv_mixer",
    )(h, g_pre, g_post, w_in, conv_k, w_out)


def _kv_kernel(h_ref, g_ref, wk_ref, wvt_ref, k_ref, vt_ref):
    xn = _rms(h_ref[...], g_ref[...]).astype(_BF16)
    k_ref[...] = _dot(xn, wk_ref[...].astype(_BF16)).astype(_BF16)
    for blk in range(vt_ref.shape[1]):
        vt_ref[0, blk] = _dot_nt(wvt_ref[...], xn[blk * KV_TILE:(blk + 1) * KV_TILE]).astype(_BF16)


def _kv_proj(h, g, w_kv, w_vt, seq_len):
    t, d = h.shape
    n_kv = seq_len // KV_TILE
    per_step = ROW_TILE // KV_TILE
    steps_per_seq = seq_len // ROW_TILE
    row = pl.BlockSpec((ROW_TILE, d), lambda i: (i, 0))
    key_half = pl.BlockSpec((d, d), lambda i: (0, 0), pipeline_mode=pl.Buffered(1))
    return pl.pallas_call(
        _kv_kernel,
        grid=(t // ROW_TILE,),
        in_specs=[row, _resident((1, d)), key_half, _resident(w_vt.shape)],
        out_specs=[row, pl.BlockSpec((1, per_step, d, KV_TILE),
                                     lambda i: (i // steps_per_seq, i % steps_per_seq, 0, 0))],
        out_shape=[jax.ShapeDtypeStruct((t, d), _BF16),
                   jax.ShapeDtypeStruct((t // seq_len, n_kv, d, KV_TILE), _BF16)],
        compiler_params=_row_params(("parallel",)),
        name="kv_proj",
    )(h, g, w_kv, w_vt)


def _q_kernel(h_ref, g_ref, wq_ref, q_ref):
    xn = _rms(h_ref[...], g_ref[...]).astype(_BF16)
    q_ref[...] = (_dot(xn, wq_ref[...].astype(_BF16)) * (HEAD_DIM ** -0.5 * LOG2_E)).astype(_BF16)


def _q_proj(h, g, w_q):
    t, d = h.shape
    row = pl.BlockSpec((ROW_TILE, d), lambda i: (i, 0))
    return pl.pallas_call(
        _q_kernel,
        grid=(t // ROW_TILE,),
        in_specs=[row, _resident((1, d)), _resident(w_q.shape)],
        out_specs=row,
        out_shape=jax.ShapeDtypeStruct((t, d), _BF16),
        compiler_params=_row_params(("parallel",)),
        name="q_proj",
    )(h, g, w_q)


def _add_key_bias(s, bias):
    return jnp.concatenate([s[:, lo:lo + LANES] + bias for lo in range(0, s.shape[1], LANES)], axis=1)


def _attn_kernel(slopes_ref, q_ref, k_ref, vt_ref, lam_ref, sg_ref, o_ref,
                 qm_ref, bias_ref, mask_ref, s_ref, acc_ref, *, lambda_init, tq):
    seq = q_ref.shape[1]
    tk = vt_ref.shape[3]
    assert tq == tk and tk % 2 == 0
    half = tk // 2
    n_q = seq // tq
    slope = slopes_ref[pl.program_id(1)] * LOG2_E

    bias_ref[...] = slope * lax.broadcasted_iota(jnp.int32, bias_ref.shape, 0).astype(_F32)
    kk = lax.broadcasted_iota(jnp.int32, (tk, tq), 0)
    qq = lax.broadcasted_iota(jnp.int32, (tk, tq), 1)
    mask_ref[...] = jnp.where(kk <= qq, slope * kk.astype(_F32), -jnp.inf)

    q = q_ref[0]
    lane = lax.broadcasted_iota(jnp.int32, q.shape, 1)
    zero = jnp.zeros_like(q)
    qm_ref[0] = jnp.where(lane < HEAD_DIM, q, zero)
    qm_ref[1] = jnp.where(lane >= HEAD_DIM, q, zero)

    ones_rows = jnp.ones((SUM_ROWS, tk), _BF16)

    lp = lam_ref[...]
    lam = (jnp.exp(jnp.sum(lp[0:1] * lp[1:2], axis=-1, keepdims=True))
           - jnp.exp(jnp.sum(lp[2:3] * lp[3:4], axis=-1, keepdims=True)) + lambda_init)

    def block_offset(qi, j):
        return slope * float(j * tk - qi * tq)

    def scores(slot, qi, j):
        kb = k_ref[0, j * tk:(j + 1) * tk, :]
        tops = []
        for c in range(2):
            qc = qm_ref[c, qi * tq:(qi + 1) * tq, :]
            if j == qi:
                upper = _dot_nt(kb[0:half], qc) + mask_ref[0:half, :]
                lower = _dot_nt(kb[half:tk], qc[half:tq]) + mask_ref[half:tk, half:tq]
                s_ref[slot, c, 0:half, :] = upper
                s_ref[slot, c, half:tk, half:tq] = lower
                left = jnp.max(upper[:, 0:half], axis=0, keepdims=True)
                right = jnp.maximum(jnp.max(upper[:, half:tq], axis=0, keepdims=True),
                                    jnp.max(lower, axis=0, keepdims=True))
                tops.append(jnp.concatenate([left, right], axis=1))
            else:
                s = _add_key_bias(_dot_nt(kb, qc), bias_ref[...])
                s_ref[slot, c] = s
                tops.append(jnp.max(s, axis=0, keepdims=True) + block_offset(qi, j))
        return tops

    def exp_pv(slot, qi, j, tops, maxima):
        vb = jnp.concatenate([vt_ref[0, j], ones_rows], axis=0)
        new_maxima = []
        for c in range(2):
            if j == qi:
                m_new = tops[c]
                p_upper = jnp.exp2(s_ref[slot, c, 0:half, :] - m_new).astype(_BF16)
                p_lower = jnp.exp2(s_ref[slot, c, half:tk, half:tq] - m_new[:, half:tq]).astype(_BF16)
                from_upper = _dot(vb[:, 0:half], p_upper)
                from_lower = _dot(vb[:, half:tk], p_lower)
                acc = jnp.concatenate([from_upper[:, 0:half], from_upper[:, half:tq] + from_lower], axis=1)
            else:
                m_new = jnp.maximum(maxima[c], tops[c])
                alpha = jnp.exp2(maxima[c] - m_new)
                p = jnp.exp2(s_ref[slot, c] - (m_new - block_offset(qi, j))).astype(_BF16)
                acc = alpha * acc_ref[qi % 2, c] + _dot(vb, p)
            acc_ref[qi % 2, c] = acc
            new_maxima.append(m_new)
        return new_maxima

    def finish(qi):
        w = HEAD_WIDTH
        acc0, acc1 = acc_ref[qi % 2, 0], acc_ref[qi % 2, 1]
        o = acc0[0:w] / acc0[w:w + 1] - lam * (acc1[0:w] / acc1[w:w + 1])
        o = o * lax.rsqrt(jnp.mean(o * o, axis=0, keepdims=True) + NORM_EPS)
        o_ref[0, qi * tq:(qi + 1) * tq, :] = (o.T * sg_ref[...] * (1.0 - lambda_init)).astype(o_ref.dtype)

    pairs = [(qi, j) for qi in range(n_q) for j in [qi] + list(range(qi))]
    tops = scores(0, *pairs[0])
    maxima = None
    for idx, (qi, j) in enumerate(pairs):
        if idx + 1 < len(pairs):
            next_tops = scores((idx + 1) % 2, *pairs[idx + 1])
        maxima = exp_pv(idx % 2, qi, j, tops, maxima)
        if j == (qi - 1 if qi else 0):
            finish(qi)
        tops = next_tops


def _diff_attention(q, k, vt, slopes, lam_params, subln_g, lambda_init):
    b, s, d = q.shape
    n_kv = vt.shape[1]
    kern = functools.partial(_attn_kernel, lambda_init=lambda_init, tq=Q_TILE)
    head_rows = pl.BlockSpec((1, s, HEAD_WIDTH), lambda bi, hi: (bi, 0, hi))
    return pl.pallas_call(
        kern,
        grid=(b, N_HEADS),
        in_specs=[
            pl.BlockSpec(memory_space=pltpu.SMEM),
            head_rows,
            head_rows,
            pl.BlockSpec((1, n_kv, HEAD_WIDTH, KV_TILE), lambda bi, hi: (bi, 0, hi, 0)),
            pl.BlockSpec(lam_params.shape, lambda bi, hi: (0, 0)),
            pl.BlockSpec(subln_g.shape, lambda bi, hi: (0, 0)),
        ],
        out_specs=head_rows,
        out_shape=jax.ShapeDtypeStruct((b, s, d), _BF16),
        scratch_shapes=[pltpu.VMEM((2, s, HEAD_WIDTH), _BF16),
                        pltpu.VMEM((KV_TILE, LANES), _F32),
                        pltpu.VMEM((KV_TILE, Q_TILE), _F32),
                        pltpu.VMEM((2, 2, KV_TILE, Q_TILE), _F32),
                        pltpu.VMEM((2, 2, HEAD_WIDTH + SUM_ROWS, Q_TILE), _F32)],
        compiler_params=pltpu.CompilerParams(
            dimension_semantics=("parallel", "parallel"),
            vmem_limit_bytes=VMEM_LIMIT_BYTES),
        name="diff_attention",
    )(slopes, q, k, vt, lam_params, subln_g)


def _oproj_kernel(h_ref, o_ref, g_ref, wo_ref, out_ref):
    h = h_ref[...]
    out_ref[...] = h + _rms(_dot(o_ref[...], wo_ref[...].astype(_BF16)), g_ref[...])


def _o_proj(h, o, g, w_o):
    t, d = h.shape
    row = pl.BlockSpec((ROW_TILE, d), lambda i: (i, 0))
    return pl.pallas_call(
        _oproj_kernel,
        grid=(t // ROW_TILE,),
        in_specs=[row, row, _resident((1, d)), _resident(w_o.shape)],
        out_specs=row,
        out_shape=jax.ShapeDtypeStruct(h.shape, _F32),
        compiler_params=_row_params(("parallel",)),
        name="o_proj",
    )(h, o, g, w_o)


def kernel(x, ffn_w_gu, ffn_w_down, norm_g, conv_w_in, conv_k, conv_w_out, kv_norm_g, w_kv,
           attn_w_q, attn_lambda, attn_subln_g, attn_w_o):
    bsz, seq, d = x.shape
    assert d == D_MODEL and seq % ROW_TILE == 0 and seq % KV_TILE == 0 and Q_TILE == KV_TILE
    depth = ffn_w_gu.shape[0]
    assert depth == 2 * N_A_LAYERS

    bf = lambda w: w.astype(_BF16)
    g = lambda l, i: norm_g[l, i].reshape(1, d)
    h = x.reshape(bsz * seq, d)
    w_gu, w_down = ffn_w_gu, ffn_w_down

    h = _ffn(h, g(0, 0), g(0, 1), w_gu, w_down, 0, 0)
    h = _conv_mixer(h, g(0, 2), g(0, 3), conv_w_in[0], conv_k[0], conv_w_out[0], seq)
    h = _ffn(h, g(0, 4), g(0, 5), w_gu, w_down, 0, 1)

    k, vt = _kv_proj(h, kv_norm_g.reshape(1, d), w_kv, bf(w_kv[:, d:].T), seq)

    h = _ffn(h, g(1, 0), g(1, 1), w_gu, w_down, 1, 0)
    q = _q_proj(h, g(1, 2), attn_w_q[0])
    lambda_init = 0.8 - 0.6 * math.exp(-0.3 * 1)
    slopes = 2.0 ** (-8.0 * jnp.arange(1, N_HEADS + 1, dtype=_F32) / N_HEADS)
    o = _diff_attention(q.reshape(bsz, seq, d), k.reshape(bsz, seq, d), vt, slopes,
                        attn_lambda[0], attn_subln_g[0].reshape(1, HEAD_WIDTH), lambda_init)
    h = _o_proj(h, o.reshape(bsz * seq, d), g(1, 3), attn_w_o[0])
    h = _ffn(h, g(1, 4), g(1, 5), w_gu, w_down, 1, 1)
    return h.reshape(bsz, seq, d)
```

```python
import functools
import math

import jax
import jax.numpy as jnp
from jax import lax
from jax.experimental import pallas as pl
from jax.experimental.pallas import tpu as pltpu

D_MODEL = 1024
D_FF = 2816
HEAD_DIM = 64
HEAD_WIDTH = 2 * HEAD_DIM
N_HEADS = D_MODEL // HEAD_WIDTH
CONV_WIDTH = 3
NORM_EPS = 1e-6
N_A_LAYERS = 1

LANES = 128
SUBLANES = 8
ROW_TILE = 1024
SUB_ROWS = 512
NORM_GROUPS = 8
FF_CHUNK = 256
CONV_CHUNK = 1024
CONV_HALO = 8
Q_TILE = 512
KV_TILE = 512
LOG2_E = math.log2(math.e)
VMEM_LIMIT_BYTES = 56 * 1024 * 1024

_F32 = jnp.float32
_BF16 = jnp.bfloat16


def _rms(x, g):
    return x * lax.rsqrt(jnp.mean(x * x, axis=-1, keepdims=True) + NORM_EPS) * g


def _dot(a, b):
    return jnp.dot(a, b, preferred_element_type=_F32)


def _dot_nt(a, b):
    return lax.dot_general(a, b, (((1,), (1,)), ((), ())), preferred_element_type=_F32)


def _resident(shape):
    return pl.BlockSpec(shape, lambda *_: (0,) * len(shape), pipeline_mode=pl.Buffered(1))


def _row_params(semantics):
    return pltpu.CompilerParams(dimension_semantics=semantics, vmem_limit_bytes=VMEM_LIMIT_BYTES)


def _max_tiles(x, rows):
    tile = None
    for i in range(0, x.shape[0], rows):
        for j in range(0, x.shape[1], LANES):
            part = x[i:i + rows, j:j + LANES]
            tile = part if tile is None else jnp.maximum(tile, part)
    return tile


def _after(x, anchor):
    rows = SUBLANES * 4 // x.dtype.itemsize
    tile = _max_tiles(anchor, rows).astype(x.dtype)
    other = jnp.tile(tile, (x.shape[0] // rows, x.shape[1] // LANES))
    return jnp.where(pl.program_id(0) >= 0, x, other)


def _ffn_kernel(h_ref, gpre_ref, gpost_ref, wgu_ref, wd_ref, o_ref, xn_ref, acc_ref):
    n_sub = h_ref.shape[0] // SUB_ROWS
    group = SUB_ROWS // NORM_GROUPS

    def rows(r, g):
        return slice(r * SUB_ROWS + g * group, r * SUB_ROWS + (g + 1) * group)

    def pre(r):
        sub = slice(r * SUB_ROWS, (r + 1) * SUB_ROWS)
        xn_ref[r] = _rms(h_ref[sub, :], gpre_ref[...]).astype(_BF16)

    def post(r, g, anchor=None):
        acc = acc_ref[r, g * group:(g + 1) * group, :]
        if anchor is not None:
            acc = _after(acc, anchor)
        o_ref[rows(r, g), :] = h_ref[rows(r, g), :] + 0.5 * _rms(acc, gpost_ref[...])

    pre(0)
    for r in range(n_sub):
        if r + 1 < n_sub:
            pre(r + 1)
        xn = xn_ref[r]
        acc = jnp.zeros((SUB_ROWS, h_ref.shape[1]), _F32)
        for c in range(D_FF // FF_CHUNK):
            lo = c * FF_CHUNK
            gate = _dot(xn, wgu_ref[:, lo:lo + FF_CHUNK].astype(_BF16))
            up = _dot(xn, wgu_ref[:, D_FF + lo:D_FF + lo + FF_CHUNK].astype(_BF16))
            act = (jax.nn.silu(gate) * up).astype(_BF16)
            g = c - 1
            if 0 <= g < NORM_GROUPS:
                if r + 1 < n_sub:
                    act = _after(act, xn_ref[r + 1, g * group:(g + 1) * group, :])
                if r > 0:
                    post(r - 1, g, anchor=acc[0:SUBLANES, :])
            acc = acc + _dot(act, wd_ref[lo:lo + FF_CHUNK, :].astype(_BF16))
        acc_ref[r] = acc
    for g in range(NORM_GROUPS):
        post(n_sub - 1, g)


def _ffn(h, g_pre, g_post, w_gu, w_down, layer, half):
    t, d = h.shape
    row = pl.BlockSpec((ROW_TILE, d), lambda i: (i, 0))

    def picked(w):
        return pl.BlockSpec((None, None) + w.shape[2:], lambda i: (layer, half, 0, 0),
                            pipeline_mode=pl.Buffered(1))

    return pl.pallas_call(
        _ffn_kernel,
        grid=(t // ROW_TILE,),
        in_specs=[row, _resident((1, d)), _resident((1, d)), picked(w_gu), picked(w_down)],
        out_specs=row,
        out_shape=jax.ShapeDtypeStruct(h.shape, _F32),
        scratch_shapes=[pltpu.VMEM((ROW_TILE // SUB_ROWS, SUB_ROWS, d), _BF16),
                        pltpu.VMEM((ROW_TILE // SUB_ROWS, SUB_ROWS, d), _F32)],
        compiler_params=_row_params(("parallel",)),
        name="ffn_half",
    )(h, g_pre, g_post, w_gu, w_down)


def _conv_kernel(h_ref, gpre_ref, gpost_ref, win_ref, ck_ref, wout_ref, o_ref, u_ref, *, tiles_per_seq):
    tm, d = h_ref.shape
    first_of_seq = pl.program_id(0) % tiles_per_seq == 0

    @pl.when(first_of_seq)
    def _():
        u_ref[0:CONV_HALO, :] = jnp.zeros((CONV_HALO, d), _F32)

    @pl.when(jnp.logical_not(first_of_seq))
    def _():
        u_ref[0:CONV_HALO, :] = u_ref[tm:tm + CONV_HALO, :]

    h = h_ref[...]
    xn = _rms(h, gpre_ref[...]).astype(_BF16)
    acc = jnp.zeros(h.shape, _F32)
    for c in range(d // CONV_CHUNK):
        lo = c * CONV_CHUNK
        cols = slice(lo, lo + CONV_CHUNK)
        b_gate = _dot(xn, win_ref[:, lo:lo + CONV_CHUNK].astype(_BF16))
        c_gate = _dot(xn, win_ref[:, d + lo:d + lo + CONV_CHUNK].astype(_BF16))
        z = _dot(xn, win_ref[:, 2 * d + lo:2 * d + lo + CONV_CHUNK].astype(_BF16))
        u = c_gate * z
        u_ref[CONV_HALO:CONV_HALO + tm, cols] = u
        conv = (ck_ref[2:3, cols] * u
                + ck_ref[1:2, cols] * u_ref[CONV_HALO - 1:CONV_HALO - 1 + tm, cols]
                + ck_ref[0:1, cols] * u_ref[CONV_HALO - 2:CONV_HALO - 2 + tm, cols])
        gated = (b_gate * conv).astype(_BF16)
        acc = acc + _dot(gated, wout_ref[lo:lo + CONV_CHUNK, :].astype(_BF16))
    o_ref[...] = h + _rms(acc, gpost_ref[...])


def _conv_mixer(h, g_pre, g_post, w_in, conv_k, w_out, seq_len):
    t, d = h.shape
    row = pl.BlockSpec((ROW_TILE, d), lambda i: (i, 0))
    kern = functools.partial(_conv_kernel, tiles_per_seq=seq_len // ROW_TILE)
    return pl.pallas_call(
        kern,
        grid=(t // ROW_TILE,),
        in_specs=[row, _resident((1, d)), _resident((1, d)), _resident(w_in.shape),
                  _resident(conv_k.shape), _resident(w_out.shape)],
        out_specs=row,
        out_shape=jax.ShapeDtypeStruct(h.shape, _F32),
        scratch_shapes=[pltpu.VMEM((ROW_TILE + CONV_HALO, d), _F32)],
        compiler_params=_row_params(("arbitrary",)),
        name="conv_mixer",
    )(h, g_pre, g_post, w_in, conv_k, w_out)


def _kv_kernel(h_ref, g_ref, wk_ref, wvt_ref, k_ref, vt_ref):
    xn = _rms(h_ref[...], g_ref[...]).astype(_BF16)
    k_ref[...] = _dot(xn, wk_ref[...].astype(_BF16)).astype(_BF16)
    for blk in range(vt_ref.shape[1]):
        vt_ref[0, blk] = _dot_nt(wvt_ref[...], xn[blk * KV_TILE:(blk + 1) * KV_TILE]).astype(_BF16)


def _kv_proj(h, g, w_kv, w_vt, seq_len):
    t, d = h.shape
    n_kv = seq_len // KV_TILE
    per_step = ROW_TILE // KV_TILE
    steps_per_seq = seq_len // ROW_TILE
    row = pl.BlockSpec((ROW_TILE, d), lambda i: (i, 0))
    key_half = pl.BlockSpec((d, d), lambda i: (0, 0), pipeline_mode=pl.Buffered(1))
    return pl.pallas_call(
        _kv_kernel,
        grid=(t // ROW_TILE,),
        in_specs=[row, _resident((1, d)), key_half, _resident(w_vt.shape)],
        out_specs=[row, pl.BlockSpec((1, per_step, d, KV_TILE),
                                     lambda i: (i // steps_per_seq, i % steps_per_seq, 0, 0))],
        out_shape=[jax.ShapeDtypeStruct((t, d), _BF16),
                   jax.ShapeDtypeStruct((t // seq_len, n_kv, d, KV_TILE), _BF16)],
        compiler_params=_row_params(("parallel",)),
        name="kv_proj",
    )(h, g, w_kv, w_vt)


def _q_kernel(h_ref, g_ref, wq_ref, q_ref):
    xn = _rms(h_ref[...], g_ref[...]).astype(_BF16)
    q_ref[...] = (_dot(xn, wq_ref[...].astype(_BF16)) * (HEAD_DIM ** -0.5 * LOG2_E)).astype(_BF16)


def _q_proj(h, g, w_q):
    t, d = h.shape
    row = pl.BlockSpec((ROW_TILE, d), lambda i: (i, 0))
    return pl.pallas_call(
        _q_kernel,
        grid=(t // ROW_TILE,),
        in_specs=[row, _resident((1, d)), _resident(w_q.shape)],
        out_specs=row,
        out_shape=jax.ShapeDtypeStruct((t, d), _BF16),
        compiler_params=_row_params(("parallel",)),
        name="q_proj",
    )(h, g, w_q)


def _add_key_bias(s, bias):
    return jnp.concatenate([s[:, lo:lo + LANES] + bias for lo in range(0, s.shape[1], LANES)], axis=1)


def _attn_kernel(slopes_ref, q_ref, k_ref, vt_ref, lam_ref, sg_ref, o_ref,
                 qm_ref, bias_ref, mask_ref, s_ref, acc_ref, *, lambda_init, tq):
    seq = q_ref.shape[1]
    tk = vt_ref.shape[3]
    assert tq == tk and tk % 2 == 0
    half = tk // 2
    n_q = seq // tq
    slope = slopes_ref[pl.program_id(1)] * LOG2_E

    bias_ref[...] = slope * lax.broadcasted_iota(jnp.int32, bias_ref.shape, 0).astype(_F32)
    kk = lax.broadcasted_iota(jnp.int32, (tk, tq), 0)
    qq = lax.broadcasted_iota(jnp.int32, (tk, tq), 1)
    mask_ref[...] = jnp.where(kk <= qq, slope * kk.astype(_F32), -jnp.inf)

    q = q_ref[0]
    lane = lax.broadcasted_iota(jnp.int32, q.shape, 1)
    zero = jnp.zeros_like(q)
    qm_ref[0] = jnp.where(lane < HEAD_DIM, q, zero)
    qm_ref[1] = jnp.where(lane >= HEAD_DIM, q, zero)

    lp = lam_ref[...]
    lam = (jnp.exp(jnp.sum(lp[0:1] * lp[1:2], axis=-1, keepdims=True))
           - jnp.exp(jnp.sum(lp[2:3] * lp[3:4], axis=-1, keepdims=True)) + lambda_init)

    def block_offset(qi, j):
        return slope * float(j * tk - qi * tq)

    def scores(slot, qi, j):
        kb = k_ref[0, j * tk:(j + 1) * tk, :]
        tops = []
        for c in range(2):
            qc = qm_ref[c, qi * tq:(qi + 1) * tq, :]
            if j == qi:
                upper = _dot_nt(kb[0:half], qc) + mask_ref[0:half, :]
                lower = _dot_nt(kb[half:tk], qc[half:tq]) + mask_ref[half:tk, half:tq]
                s_ref[slot, c, 0:half, :] = upper
                s_ref[slot, c, half:tk, half:tq] = lower
                left = jnp.max(upper[:, 0:half], axis=0, keepdims=True)
                right = jnp.maximum(jnp.max(upper[:, half:tq], axis=0, keepdims=True),
                                    jnp.max(lower, axis=0, keepdims=True))
                tops.append(jnp.concatenate([left, right], axis=1))
            else:
                s = _add_key_bias(_dot_nt(kb, qc), bias_ref[...])
                s_ref[slot, c] = s
                tops.append(jnp.max(s, axis=0, keepdims=True) + block_offset(qi, j))
        return tops

    def exp_pv(slot, qi, j, tops, state):
        vb = vt_ref[0, j]
        new_maxima, new_sums = [], []
        for c in range(2):
            if j == qi:
                m_new = tops[c]
                p_upper = jnp.exp2(s_ref[slot, c, 0:half, :] - m_new)
                p_lower = jnp.exp2(s_ref[slot, c, half:tk, half:tq] - m_new[:, half:tq])
                l_upper = jnp.sum(p_upper, axis=0, keepdims=True)
                l_lower = jnp.sum(p_lower, axis=0, keepdims=True)
                l_new = jnp.concatenate([l_upper[:, 0:half], l_upper[:, half:tq] + l_lower], axis=1)
                from_upper = _dot(vb[:, 0:half], p_upper.astype(_BF16))
                from_lower = _dot(vb[:, half:tk], p_lower.astype(_BF16))
                acc = jnp.concatenate([from_upper[:, 0:half], from_upper[:, half:tq] + from_lower], axis=1)
            else:
                maxima, sums = state
                m_new = jnp.maximum(maxima[c], tops[c])
                alpha = jnp.exp2(maxima[c] - m_new)
                p = jnp.exp2(s_ref[slot, c] - (m_new - block_offset(qi, j)))
                l_new = alpha * sums[c] + jnp.sum(p, axis=0, keepdims=True)
                acc = alpha * acc_ref[qi % 2, c] + _dot(vb, p.astype(_BF16))
            acc_ref[qi % 2, c] = acc
            new_maxima.append(m_new)
            new_sums.append(l_new)
        return new_maxima, new_sums

    def finish(qi, sums):
        o = acc_ref[qi % 2, 0] / sums[0] - lam * (acc_ref[qi % 2, 1] / sums[1])
        o = o * lax.rsqrt(jnp.mean(o * o, axis=0, keepdims=True) + NORM_EPS)
        o_ref[0, qi * tq:(qi + 1) * tq, :] = (o.T * sg_ref[...] * (1.0 - lambda_init)).astype(o_ref.dtype)

    pairs = [(qi, j) for qi in range(n_q) for j in [qi] + list(range(qi))]
    tops = scores(0, *pairs[0])
    state = None
    for idx, (qi, j) in enumerate(pairs):
        if idx + 1 < len(pairs):
            next_tops = scores((idx + 1) % 2, *pairs[idx + 1])
        state = exp_pv(idx % 2, qi, j, tops, state)
        if j == (qi - 1 if qi else 0):
            finish(qi, state[1])
        tops = next_tops


def _diff_attention(q, k, vt, slopes, lam_params, subln_g, lambda_init):
    b, s, d = q.shape
    n_kv = vt.shape[1]
    kern = functools.partial(_attn_kernel, lambda_init=lambda_init, tq=Q_TILE)
    head_rows = pl.BlockSpec((1, s, HEAD_WIDTH), lambda bi, hi: (bi, 0, hi))
    return pl.pallas_call(
        kern,
        grid=(b, N_HEADS),
        in_specs=[
            pl.BlockSpec(memory_space=pltpu.SMEM),
            head_rows,
            head_rows,
            pl.BlockSpec((1, n_kv, HEAD_WIDTH, KV_TILE), lambda bi, hi: (bi, 0, hi, 0)),
            pl.BlockSpec(lam_params.shape, lambda bi, hi: (0, 0)),
            pl.BlockSpec(subln_g.shape, lambda bi, hi: (0, 0)),
        ],
        out_specs=head_rows,
        out_shape=jax.ShapeDtypeStruct((b, s, d), _BF16),
        scratch_shapes=[pltpu.VMEM((2, s, HEAD_WIDTH), _BF16),
                        pltpu.VMEM((KV_TILE, LANES), _F32),
                        pltpu.VMEM((KV_TILE, Q_TILE), _F32),
                        pltpu.VMEM((2, 2, KV_TILE, Q_TILE), _F32),
                        pltpu.VMEM((2, 2, HEAD_WIDTH, Q_TILE), _F32)],
        compiler_params=pltpu.CompilerParams(
            dimension_semantics=("parallel", "parallel"),
            vmem_limit_bytes=VMEM_LIMIT_BYTES),
        name="diff_attention",
    )(slopes, q, k, vt, lam_params, subln_g)


def _oproj_kernel(h_ref, o_ref, g_ref, wo_ref, out_ref):
    h = h_ref[...]
    out_ref[...] = h + _rms(_dot(o_ref[...], wo_ref[...].astype(_BF16)), g_ref[...])


def _o_proj(h, o, g, w_o):
    t, d = h.shape
    row = pl.BlockSpec((ROW_TILE, d), lambda i: (i, 0))
    return pl.pallas_call(
        _oproj_kernel,
        grid=(t // ROW_TILE,),
        in_specs=[row, row, _resident((1, d)), _resident(w_o.shape)],
        out_specs=row,
        out_shape=jax.ShapeDtypeStruct(h.shape, _F32),
        compiler_params=_row_params(("parallel",)),
        name="o_proj",
    )(h, o, g, w_o)


def kernel(x, ffn_w_gu, ffn_w_down, norm_g, conv_w_in, conv_k, conv_w_out, kv_norm_g, w_kv,
           attn_w_q, attn_lambda, attn_subln_g, attn_w_o):
    bsz, seq, d = x.shape
    assert d == D_MODEL and seq % ROW_TILE == 0 and seq % KV_TILE == 0 and Q_TILE == KV_TILE
    depth = ffn_w_gu.shape[0]
    assert depth == 2 * N_A_LAYERS

    bf = lambda w: w.astype(_BF16)
    g = lambda l, i: norm_g[l, i].reshape(1, d)
    h = x.reshape(bsz * seq, d)
    w_gu, w_down = ffn_w_gu, ffn_w_down

    h = _ffn(h, g(0, 0), g(0, 1), w_gu, w_down, 0, 0)
    h = _conv_mixer(h, g(0, 2), g(0, 3), conv_w_in[0], conv_k[0], conv_w_out[0], seq)
    h = _ffn(h, g(0, 4), g(0, 5), w_gu, w_down, 0, 1)

    k, vt = _kv_proj(h, kv_norm_g.reshape(1, d), w_kv, bf(w_kv[:, d:].T), seq)

    h = _ffn(h, g(1, 0), g(1, 1), w_gu, w_down, 1, 0)
    q = _q_proj(h, g(1, 2), attn_w_q[0])
    lambda_init = 0.8 - 0.6 * math.exp(-0.3 * 1)
    slopes = 2.0 ** (-8.0 * jnp.arange(1, N_HEADS + 1, dtype=_F32) / N_HEADS)
    o = _diff_attention(q.reshape(bsz, seq, d), k.reshape(bsz, seq, d), vt, slopes,
                        attn_lambda[0], attn_subln_g[0].reshape(1, HEAD_WIDTH), lambda_init)
    h = _o_proj(h, o.reshape(bsz * seq, d), g(1, 3), attn_w_o[0])
    h = _ffn(h, g(1, 4), g(1, 5), w_gu, w_down, 1, 1)
    return h.reshape(bsz, seq, d)
```

```python
import functools
import math

import jax
import jax.numpy as jnp
from jax import lax
from jax.experimental import pallas as pl
from jax.experimental.pallas import tpu as pltpu

D_MODEL = 1024
D_FF = 2816
HEAD_DIM = 64
HEAD_WIDTH = 2 * HEAD_DIM
N_HEADS = D_MODEL // HEAD_WIDTH
CONV_WIDTH = 3
NORM_EPS = 1e-6
N_A_LAYERS = 1

LANES = 128
SUBLANES = 8
ROW_TILE = 1024
SUB_ROWS = 512
NORM_GROUPS = 8
FF_CHUNK = 256
CONV_CHUNK = 1024
CONV_HALO = 8
Q_TILE = 512
KV_TILE = 512
SUM_ROWS = 16
LOG2_E = math.log2(math.e)
VMEM_LIMIT_BYTES = 56 * 1024 * 1024

_F32 = jnp.float32
_BF16 = jnp.bfloat16


def _rms(x, g):
    return x * lax.rsqrt(jnp.mean(x * x, axis=-1, keepdims=True) + NORM_EPS) * g


def _dot(a, b):
    return jnp.dot(a, b, preferred_element_type=_F32)


def _dot_nt(a, b):
    return lax.dot_general(a, b, (((1,), (1,)), ((), ())), preferred_element_type=_F32)


def _resident(shape):
    return pl.BlockSpec(shape, lambda *_: (0,) * len(shape), pipeline_mode=pl.Buffered(1))


def _row_params(semantics):
    return pltpu.CompilerParams(dimension_semantics=semantics, vmem_limit_bytes=VMEM_LIMIT_BYTES)


def _max_tiles(x, rows):
    tile = None
    for i in range(0, x.shape[0], rows):
        for j in range(0, x.shape[1], LANES):
            part = x[i:i + rows, j:j + LANES]
            tile = part if tile is None else jnp.maximum(tile, part)
    return tile


def _after(x, anchor):
    rows = SUBLANES * 4 // x.dtype.itemsize
    tile = _max_tiles(anchor, rows).astype(x.dtype)
    other = jnp.tile(tile, (x.shape[0] // rows, x.shape[1] // LANES))
    return jnp.where(pl.program_id(0) >= 0, x, other)


def _ffn_kernel(h_ref, gpre_ref, gpost_ref, wgu_ref, wd_ref, o_ref, xn_ref, acc_ref):
    n_sub = h_ref.shape[0] // SUB_ROWS
    group = SUB_ROWS // NORM_GROUPS

    def rows(r, g):
        return slice(r * SUB_ROWS + g * group, r * SUB_ROWS + (g + 1) * group)

    def pre(r):
        sub = slice(r * SUB_ROWS, (r + 1) * SUB_ROWS)
        xn_ref[r] = _rms(h_ref[sub, :], gpre_ref[...]).astype(_BF16)

    def post(r, g, anchor=None):
        acc = acc_ref[r, g * group:(g + 1) * group, :]
        if anchor is not None:
            acc = _after(acc, anchor)
        o_ref[rows(r, g), :] = h_ref[rows(r, g), :] + 0.5 * _rms(acc, gpost_ref[...])

    pre(0)
    for r in range(n_sub):
        if r + 1 < n_sub:
            pre(r + 1)
        xn = xn_ref[r]
        acc = jnp.zeros((SUB_ROWS, h_ref.shape[1]), _F32)
        for c in range(D_FF // FF_CHUNK):
            lo = c * FF_CHUNK
            gate = _dot(xn, wgu_ref[:, lo:lo + FF_CHUNK].astype(_BF16))
            up = _dot(xn, wgu_ref[:, D_FF + lo:D_FF + lo + FF_CHUNK].astype(_BF16))
            act = (jax.nn.silu(gate) * up).astype(_BF16)
            g = c - 1
            if 0 <= g < NORM_GROUPS:
                if r + 1 < n_sub:
                    act = _after(act, xn_ref[r + 1, g * group:(g + 1) * group, :])
                if r > 0:
                    post(r - 1, g, anchor=acc[0:SUBLANES, :])
            acc = acc + _dot(act, wd_ref[lo:lo + FF_CHUNK, :].astype(_BF16))
        acc_ref[r] = acc
    for g in range(NORM_GROUPS):
        post(n_sub - 1, g)


def _ffn(h, g_pre, g_post, w_gu, w_down, layer, half):
    t, d = h.shape
    row = pl.BlockSpec((ROW_TILE, d), lambda i: (i, 0))

    def picked(w):
        return pl.BlockSpec((None, None) + w.shape[2:], lambda i: (layer, half, 0, 0),
                            pipeline_mode=pl.Buffered(1))

    return pl.pallas_call(
        _ffn_kernel,
        grid=(t // ROW_TILE,),
        in_specs=[row, _resident((1, d)), _resident((1, d)), picked(w_gu), picked(w_down)],
        out_specs=row,
        out_shape=jax.ShapeDtypeStruct(h.shape, _F32),
        scratch_shapes=[pltpu.VMEM((ROW_TILE // SUB_ROWS, SUB_ROWS, d), _BF16),
                        pltpu.VMEM((ROW_TILE // SUB_ROWS, SUB_ROWS, d), _F32)],
        compiler_params=_row_params(("parallel",)),
        name="ffn_half",
    )(h, g_pre, g_post, w_gu, w_down)


def _conv_kernel(h_ref, gpre_ref, gpost_ref, win_ref, ck_ref, wout_ref, o_ref, u_ref, *, tiles_per_seq):
    tm, d = h_ref.shape
    first_of_seq = pl.program_id(0) % tiles_per_seq == 0

    @pl.when(first_of_seq)
    def _():
        u_ref[0:CONV_HALO, :] = jnp.zeros((CONV_HALO, d), _F32)

    @pl.when(jnp.logical_not(first_of_seq))
    def _():
        u_ref[0:CONV_HALO, :] = u_ref[tm:tm + CONV_HALO, :]

    h = h_ref[...]
    xn = _rms(h, gpre_ref[...]).astype(_BF16)
    acc = jnp.zeros(h.shape, _F32)
    for c in range(d // CONV_CHUNK):
        lo = c * CONV_CHUNK
        cols = slice(lo, lo + CONV_CHUNK)
        b_gate = _dot(xn, win_ref[:, lo:lo + CONV_CHUNK].astype(_BF16))
        c_gate = _dot(xn, win_ref[:, d + lo:d + lo + CONV_CHUNK].astype(_BF16))
        z = _dot(xn, win_ref[:, 2 * d + lo:2 * d + lo + CONV_CHUNK].astype(_BF16))
        u = c_gate * z
        u_ref[CONV_HALO:CONV_HALO + tm, cols] = u
        conv = (ck_ref[2:3, cols] * u
                + ck_ref[1:2, cols] * u_ref[CONV_HALO - 1:CONV_HALO - 1 + tm, cols]
                + ck_ref[0:1, cols] * u_ref[CONV_HALO - 2:CONV_HALO - 2 + tm, cols])
        gated = (b_gate * conv).astype(_BF16)
        acc = acc + _dot(gated, wout_ref[lo:lo + CONV_CHUNK, :].astype(_BF16))
    o_ref[...] = h + _rms(acc, gpost_ref[...])


def _conv_mixer(h, g_pre, g_post, w_in, conv_k, w_out, seq_len):
    t, d = h.shape
    row = pl.BlockSpec((ROW_TILE, d), lambda i: (i, 0))
    kern = functools.partial(_conv_kernel, tiles_per_seq=seq_len // ROW_TILE)
    return pl.pallas_call(
        kern,
        grid=(t // ROW_TILE,),
        in_specs=[row, _resident((1, d)), _resident((1, d)), _resident(w_in.shape),
                  _resident(conv_k.shape), _resident(w_out.shape)],
        out_specs=row,
        out_shape=jax.ShapeDtypeStruct(h.shape, _F32),
        scratch_shapes=[pltpu.VMEM((ROW_TILE + CONV_HALO, d), _F32)],
        compiler_params=_row_params(("arbitrary",)),
        name="conv_mixer",
    )(h, g_pre, g_post, w_in, conv_k, w_out)


def _kv_kernel(h_ref, g_ref, wk_ref, wvt_ref, k_ref, vt_ref):
    xn = _rms(h_ref[...], g_ref[...]).astype(_BF16)
    k_ref[...] = _dot(xn, wk_ref[...].astype(_BF16)).astype(_BF16)
    for blk in range(vt_ref.shape[1]):
        vt_ref[0, blk] = _dot_nt(wvt_ref[...], xn[blk * KV_TILE:(blk + 1) * KV_TILE]).astype(_BF16)


def _kv_proj(h, g, w_kv, w_vt, seq_len):
    t, d = h.shape
    n_kv = seq_len // KV_TILE
    per_step = ROW_TILE // KV_TILE
    steps_per_seq = seq_len // ROW_TILE
    row = pl.BlockSpec((ROW_TILE, d), lambda i: (i, 0))
    key_half = pl.BlockSpec((d, d), lambda i: (0, 0), pipeline_mode=pl.Buffered(1))
    return pl.pallas_call(
        _kv_kernel,
        grid=(t // ROW_TILE,),
        in_specs=[row, _resident((1, d)), key_half, _resident(w_vt.shape)],
        out_specs=[row, pl.BlockSpec((1, per_step, d, KV_TILE),
                                     lambda i: (i // steps_per_seq, i % steps_per_seq, 0, 0))],
        out_shape=[jax.ShapeDtypeStruct((t, d), _BF16),
                   jax.ShapeDtypeStruct((t // seq_len, n_kv, d, KV_TILE), _BF16)],
        compiler_params=_row_params(("parallel",)),
        name="kv_proj",
    )(h, g, w_kv, w_vt)


def _q_kernel(h_ref, g_ref, wq_ref, q_ref):
    xn = _rms(h_ref[...], g_ref[...]).astype(_BF16)
    q_ref[...] = (_dot(xn, wq_ref[...].astype(_BF16)) * (HEAD_DIM ** -0.5 * LOG2_E)).astype(_BF16)


def _q_proj(h, g, w_q):
    t, d = h.shape
    row = pl.BlockSpec((ROW_TILE, d), lambda i: (i, 0))
    return pl.pallas_call(
        _q_kernel,
        grid=(t // ROW_TILE,),
        in_specs=[row, _resident((1, d)), _resident(w_q.shape)],
        out_specs=row,
        out_shape=jax.ShapeDtypeStruct((t, d), _BF16),
        compiler_params=_row_params(("parallel",)),
        name="q_proj",
    )(h, g, w_q)


def _attn_kernel(slopes_ref, q_ref, k_ref, vt_ref, lam_ref, sg_ref, o_ref,
                 qm_ref, kpos_ref, mask_ref, s_ref, acc_ref, *, lambda_init, tq):
    seq = q_ref.shape[1]
    tk = vt_ref.shape[3]
    assert tq == tk and tk % 2 == 0
    half = tk // 2
    n_q = seq // tq
    w = HEAD_WIDTH
    slope = slopes_ref[pl.program_id(1)] * LOG2_E

    kk = lax.broadcasted_iota(jnp.int32, (tk, tq), 0)
    qq = lax.broadcasted_iota(jnp.int32, (tk, tq), 1)
    mask_ref[...] = jnp.where(kk <= qq, slope * kk.astype(_F32), -jnp.inf)

    row = lax.broadcasted_iota(jnp.int32, (tk, w), 0)
    col = lax.broadcasted_iota(jnp.int32, (tk, w), 1)
    kpos = jnp.where(col < 3, (row // 16) * 16, jnp.where(col < 6, row % 16, 0))
    kpos_ref[...] = kpos.astype(_F32).astype(_BF16)

    piece = jnp.full((16, w), slope, _F32)
    hi = piece.astype(_BF16).astype(_F32)
    mid = (piece - hi).astype(_BF16).astype(_F32)
    lo = (piece - hi - mid).astype(_BF16).astype(_F32)
    lane16 = lax.broadcasted_iota(jnp.int32, (16, w), 1)
    pieces = jnp.where(lane16 % 3 == 0, hi, jnp.where(lane16 % 3 == 1, mid, lo))
    pieces = jnp.where(lane16 < 6, pieces, 0.0).astype(_BF16)

    q = q_ref[0]
    lane = lax.broadcasted_iota(jnp.int32, q.shape, 1)
    zero = jnp.zeros_like(q)
    qm_ref[0, :, 0:w] = jnp.where(lane < HEAD_DIM, q, zero)
    qm_ref[1, :, 0:w] = jnp.where(lane >= HEAD_DIM, q, zero)
    slope_cols = jnp.tile(pieces, (seq // 16, 1))
    qm_ref[0, :, w:2 * w] = slope_cols
    qm_ref[1, :, w:2 * w] = slope_cols

    ones_rows = jnp.ones((SUM_ROWS, tk), _BF16)

    lp = lam_ref[...]
    lam = (jnp.exp(jnp.sum(lp[0:1] * lp[1:2], axis=-1, keepdims=True))
           - jnp.exp(jnp.sum(lp[2:3] * lp[3:4], axis=-1, keepdims=True)) + lambda_init)

    def block_offset(qi, j):
        return slope * float(j * tk - qi * tq)

    def scores(slot, qi, j):
        kb = k_ref[0, j * tk:(j + 1) * tk, :]
        tops = []
        for c in range(2):
            if j == qi:
                qc = qm_ref[c, qi * tq:(qi + 1) * tq, 0:w]
                upper = _dot_nt(kb[0:half], qc) + mask_ref[0:half, :]
                lower = _dot_nt(kb[half:tk], qc[half:tq]) + mask_ref[half:tk, half:tq]
                s_ref[slot, c, 0:half, :] = upper
                s_ref[slot, c, half:tk, half:tq] = lower
                left = jnp.max(upper[:, 0:half], axis=0, keepdims=True)
                right = jnp.maximum(jnp.max(upper[:, half:tq], axis=0, keepdims=True),
                                    jnp.max(lower, axis=0, keepdims=True))
                tops.append(jnp.concatenate([left, right], axis=1))
            else:
                keys = jnp.concatenate([kb, kpos_ref[...]], axis=1)
                s = _dot_nt(keys, qm_ref[c, qi * tq:(qi + 1) * tq, :])
                s_ref[slot, c] = s
                tops.append(jnp.max(s, axis=0, keepdims=True) + block_offset(qi, j))
        return tops

    def exp_pv(slot, qi, j, tops, maxima):
        vb = jnp.concatenate([vt_ref[0, j], ones_rows], axis=0)
        new_maxima = []
        for c in range(2):
            if j == qi:
                m_new = tops[c]
                p_upper = jnp.exp2(s_ref[slot, c, 0:half, :] - m_new).astype(_BF16)
                p_lower = jnp.exp2(s_ref[slot, c, half:tk, half:tq] - m_new[:, half:tq]).astype(_BF16)
                from_upper = _dot(vb[:, 0:half], p_upper)
                from_lower = _dot(vb[:, half:tk], p_lower)
                acc = jnp.concatenate([from_upper[:, 0:half], from_upper[:, half:tq] + from_lower], axis=1)
            else:
                m_new = jnp.maximum(maxima[c], tops[c])
                alpha = jnp.exp2(maxima[c] - m_new)
                p = jnp.exp2(s_ref[slot, c] - (m_new - block_offset(qi, j))).astype(_BF16)
                acc = alpha * acc_ref[qi % 2, c] + _dot(vb, p)
            acc_ref[qi % 2, c] = acc
            new_maxima.append(m_new)
        return new_maxima

    def finish(qi):
        acc0, acc1 = acc_ref[qi % 2, 0], acc_ref[qi % 2, 1]
        o = acc0[0:w] / acc0[w:w + 1] - lam * (acc1[0:w] / acc1[w:w + 1])
        o = o * lax.rsqrt(jnp.mean(o * o, axis=0, keepdims=True) + NORM_EPS)
        o_ref[0, qi * tq:(qi + 1) * tq, :] = (o.T * sg_ref[...] * (1.0 - lambda_init)).astype(o_ref.dtype)

    pairs = [(qi, j) for qi in range(n_q) for j in [qi] + list(range(qi))]
    tops = scores(0, *pairs[0])
    maxima = None
    for idx, (qi, j) in enumerate(pairs):
        if idx + 1 < len(pairs):
            next_tops = scores((idx + 1) % 2, *pairs[idx + 1])
        maxima = exp_pv(idx % 2, qi, j, tops, maxima)
        if j == (qi - 1 if qi else 0):
            finish(qi)
        tops = next_tops


def _diff_attention(q, k, vt, slopes, lam_params, subln_g, lambda_init):
    b, s, d = q.shape
    n_kv = vt.shape[1]
    kern = functools.partial(_attn_kernel, lambda_init=lambda_init, tq=Q_TILE)
    head_rows = pl.BlockSpec((1, s, HEAD_WIDTH), lambda bi, hi: (bi, 0, hi))
    return pl.pallas_call(
        kern,
        grid=(b, N_HEADS),
        in_specs=[
            pl.BlockSpec(memory_space=pltpu.SMEM),
            head_rows,
            head_rows,
            pl.BlockSpec((1, n_kv, HEAD_WIDTH, KV_TILE), lambda bi, hi: (bi, 0, hi, 0)),
            pl.BlockSpec(lam_params.shape, lambda bi, hi: (0, 0)),
            pl.BlockSpec(subln_g.shape, lambda bi, hi: (0, 0)),
        ],
        out_specs=head_rows,
        out_shape=jax.ShapeDtypeStruct((b, s, d), _BF16),
        scratch_shapes=[pltpu.VMEM((2, s, 2 * HEAD_WIDTH), _BF16),
                        pltpu.VMEM((KV_TILE, HEAD_WIDTH), _BF16),
                        pltpu.VMEM((KV_TILE, Q_TILE), _F32),
                        pltpu.VMEM((2, 2, KV_TILE, Q_TILE), _F32),
                        pltpu.VMEM((2, 2, HEAD_WIDTH + SUM_ROWS, Q_TILE), _F32)],
        compiler_params=pltpu.CompilerParams(
            dimension_semantics=("parallel", "parallel"),
            vmem_limit_bytes=VMEM_LIMIT_BYTES),
        name="diff_attention",
    )(slopes, q, k, vt, lam_params, subln_g)


def _oproj_kernel(h_ref, o_ref, g_ref, wo_ref, out_ref):
    h = h_ref[...]
    out_ref[...] = h + _rms(_dot(o_ref[...], wo_ref[...].astype(_BF16)), g_ref[...])


def _o_proj(h, o, g, w_o):
    t, d = h.shape
    row = pl.BlockSpec((ROW_TILE, d), lambda i: (i, 0))
    return pl.pallas_call(
        _oproj_kernel,
        grid=(t // ROW_TILE,),
        in_specs=[row, row, _resident((1, d)), _resident(w_o.shape)],
        out_specs=row,
        out_shape=jax.ShapeDtypeStruct(h.shape, _F32),
        compiler_params=_row_params(("parallel",)),
        name="o_proj",
    )(h, o, g, w_o)


def kernel(x, ffn_w_gu, ffn_w_down, norm_g, conv_w_in, conv_k, conv_w_out, kv_norm_g, w_kv,
           attn_w_q, attn_lambda, attn_subln_g, attn_w_o):
    bsz, seq, d = x.shape
    assert d == D_MODEL and seq % ROW_TILE == 0 and seq % KV_TILE == 0 and Q_TILE == KV_TILE
    depth = ffn_w_gu.shape[0]
    assert depth == 2 * N_A_LAYERS

    bf = lambda w: w.astype(_BF16)
    g = lambda l, i: norm_g[l, i].reshape(1, d)
    h = x.reshape(bsz * seq, d)
    w_gu, w_down = ffn_w_gu, ffn_w_down

    h = _ffn(h, g(0, 0), g(0, 1), w_gu, w_down, 0, 0)
    h = _conv_mixer(h, g(0, 2), g(0, 3), conv_w_in[0], conv_k[0], conv_w_out[0], seq)
    h = _ffn(h, g(0, 4), g(0, 5), w_gu, w_down, 0, 1)

    k, vt = _kv_proj(h, kv_norm_g.reshape(1, d), w_kv, bf(w_kv[:, d:].T), seq)

    h = _ffn(h, g(1, 0), g(1, 1), w_gu, w_down, 1, 0)
    q = _q_proj(h, g(1, 2), attn_w_q[0])
    lambda_init = 0.8 - 0.6 * math.exp(-0.3 * 1)
    slopes = 2.0 ** (-8.0 * jnp.arange(1, N_HEADS + 1, dtype=_F32) / N_HEADS)
    o = _diff_attention(q.reshape(bsz, seq, d), k.reshape(bsz, seq, d), vt, slopes,
                        attn_lambda[0], attn_subln_g[0].reshape(1, HEAD_WIDTH), lambda_init)
    h = _o_proj(h, o.reshape(bsz * seq, d), g(1, 3), attn_w_o[0])
    h = _ffn(h, g(1, 4), g(1, 5), w_gu, w_down, 1, 1)
    return h.reshape(bsz, seq, d)
```
